```python
import jax, jax.numpy as jnp
from jax import lax
import numpy as np

D_MODEL = 2048
BATCH = 2
SEQ = 4096
DEPTH = 2
DEC_BATCH = 128
DEC_SEQ = 1
PAST_LEN = 8192
PAGE_SIZE = 128

H_A = 4
DQK_A = 128
DV_A = 256
MLSTM_CHUNK = 64
FORGET_BIAS = 3.0
H_B = 8
D_B = 128
SB_SCALE = D_B ** -0.5
H_C = 16
Q_LORA = 512
KV_LORA = 512
NOPE_DIM = 128
ROPE_DIM = 64
V_C = 128
ROPE_THETA = 10000.0
MLA_SCALE = (NOPE_DIM + ROPE_DIM) ** -0.5
IN_AB = 2 * H_A * DQK_A + 2 * H_A * DV_A + 2 * H_A + 3 * H_B * D_B
MIX_AB = H_A * DV_A + H_B * D_B
IN_C = Q_LORA + KV_LORA + ROPE_DIM
MIX_C = H_C * V_C
FFN_HIDDEN = ((8 * D_MODEL + 3 * 256 - 1) // (3 * 256)) * 256
PLE_DIM = 256
Q_BLOCK = 128
N_EVEN = (DEPTH + 1) // 2
N_ODD = DEPTH // 2
EPS = 1e-6

kernel_name = 'hybrid_mlstm_stickbreak_mla_decode_step'


def rmsnorm(x, g):
    xf = x.astype(jnp.float32)
    y = xf * lax.rsqrt(jnp.mean(xf * xf, axis=-1, keepdims=True) + EPS)
    return (y * g.astype(jnp.float32)).astype(x.dtype)


def rope(x, pos):
    half = x.shape[-1] // 2
    freqs = ROPE_THETA ** (-jnp.arange(half, dtype=jnp.float32) / half)
    ang = pos.astype(jnp.float32)[:, None] * freqs[None, :]
    cos = jnp.cos(ang)[None, :, None, :]
    sin = jnp.sin(ang)[None, :, None, :]
    xf = x.astype(jnp.float32)
    x1, x2 = xf[..., :half], xf[..., half:]
    return jnp.concatenate([x1 * cos - x2 * sin, x1 * sin + x2 * cos], axis=-1).astype(x.dtype)


def gather_pages(pool, layer, page_table):
    rows = pool[layer, page_table]
    return rows.reshape((page_table.shape[0], -1) + rows.shape[3:])


def sweep_query_blocks(fn, q, q_pos):
    B, T = q.shape[0], q.shape[1]
    if T > Q_BLOCK and T % Q_BLOCK == 0:
        nb = T // Q_BLOCK
        qb = jnp.moveaxis(q.reshape((B, nb, Q_BLOCK) + q.shape[2:]), 1, 0)
        out = lax.map(lambda a: fn(a[0], a[1]), (qb, q_pos.reshape(nb, Q_BLOCK)))
        out = jnp.moveaxis(out, 0, 1)
        return out.reshape((B, T) + out.shape[3:])
    return fn(q, q_pos)


def stick_breaking(q, k, v, q_pos, k_pos):
    def attend(qb, pb):
        z = jnp.einsum('bthd,bshd->bhts', qb, k).astype(jnp.float32) * SB_SCALE
        mask = k_pos[None, :] < pb[:, None]
        log_keep = jnp.where(mask, jax.nn.log_sigmoid(-z), 0.0)
        log_after = lax.cumsum(log_keep, axis=3, reverse=True) - log_keep
        a = jnp.where(mask, jnp.exp(jax.nn.log_sigmoid(z) + log_after), 0.0)
        return jnp.einsum('bhts,bshd->bthd', a.astype(v.dtype), v)
    return sweep_query_blocks(attend, q, q_pos)


def mlstm_chunk(carry, inp):
    c, n, m = carry
    q, k, v, ig, lf = inp
    L = q.shape[2]
    b = jnp.cumsum(lf, axis=-1)
    causal = jnp.tril(jnp.ones((L, L), dtype=bool))
    d = jnp.where(causal, b[..., :, None] - b[..., None, :] + ig[..., None, :], -jnp.inf)
    g = b + m[..., None]
    m_t = jnp.maximum(g, jnp.max(d, axis=-1))
    w_in = jnp.exp(g - m_t)
    s = jnp.einsum('bhtd,bhsd->bhts', q, k) * jnp.exp(d - m_t[..., None])
    num = w_in[..., None] * jnp.einsum('bhtd,bhde->bhte', q, c) + jnp.einsum('bhts,bhse->bhte', s, v)
    den = w_in * jnp.einsum('bhtd,bhd->bht', q, n) + jnp.sum(s, axis=-1)
    h = num / jnp.maximum(jnp.abs(den), jnp.exp(-m_t))[..., None]
    m_new = m_t[..., -1]
    decay = jnp.exp(g[..., -1] - m_new)
    w_k = jnp.exp(b[..., -1:] - b + ig - m_new[..., None])
    c_new = decay[..., None, None] * c + jnp.einsum('bhs,bhsd,bhse->bhde', w_k, k, v)
    n_new = decay[..., None] * n + jnp.einsum('bhs,bhsd->bhd', w_k, k)
    return (c_new, n_new, m_new), h


def mlstm_scan(q, k, v, ig, lf, c0, n0, m0):
    B, H, T = ig.shape
    L = MLSTM_CHUNK if T % MLSTM_CHUNK == 0 else T
    nc = T // L

    def to_chunks(a):
        a = a.astype(jnp.float32)
        return jnp.moveaxis(a.reshape((B, H, nc, L) + a.shape[3:]), 2, 0)

    state, h = lax.scan(mlstm_chunk, (c0, n0, m0), tuple(to_chunks(a) for a in (q, k, v, ig, lf)))
    h = jnp.moveaxis(h, 0, 2).reshape(B, H, T, DV_A)
    return h, state


def mlstm_sb_mixer(hn, w_in, b_gates, g_head, w_out, state0, kv_past, pos):
    B, T, _ = hn.shape
    proj = hn @ w_in
    sizes = (H_A * DQK_A, H_A * DQK_A, H_A * DV_A, H_A * DV_A, 2 * H_A, H_B * D_B, H_B * D_B, H_B * D_B)
    parts, start = [], 0
    for sz in sizes:
        parts.append(proj[..., start:start + sz])
        start += sz
    qa, ka, va, oa, gates, qb, kb, vb = parts
    gates = gates.astype(jnp.float32) + b_gates.astype(jnp.float32)
    ig = jnp.swapaxes(gates[..., :H_A], 1, 2)
    lf = jnp.swapaxes(jax.nn.log_sigmoid(gates[..., H_A:]), 1, 2)

    def heads(a, dim):
        return jnp.swapaxes(a.reshape(B, T, H_A, dim), 1, 2)

    ha, state = mlstm_scan(heads(qa, DQK_A), heads(ka, DQK_A) * (DQK_A ** -0.5), heads(va, DV_A), ig, lf, *state0)
    ha = jnp.swapaxes(ha, 1, 2).astype(hn.dtype)
    ha = rmsnorm(ha, g_head.reshape(H_A, DV_A)) * jax.nn.sigmoid(oa).reshape(B, T, H_A, DV_A)
    qb = qb.reshape(B, T, H_B, D_B)
    kb = kb.reshape(B, T, H_B, D_B)
    vb = vb.reshape(B, T, H_B, D_B)
    if kv_past is None:
        k_all, v_all, k_pos = kb, vb, pos
    else:
        k_past, v_past = kv_past
        k_all = jnp.concatenate([k_past, kb], axis=1)
        v_all = jnp.concatenate([v_past, vb], axis=1)
        k_pos = jnp.concatenate([jnp.arange(k_past.shape[1], dtype=jnp.int32), pos])
    hb = stick_breaking(qb, k_all, v_all, pos, k_pos)
    mix = jnp.concatenate([ha.reshape(B, T, -1), hb.reshape(B, T, -1)], axis=-1) @ w_out
    return mix, state, kb, vb


def mla_mixer(hn, w_in, g_q, g_kv, w_q_up, w_kv_up, w_out, latent_past, pos):
    B, T, _ = hn.shape
    proj = hn @ w_in
    cq = rmsnorm(proj[..., :Q_LORA], g_q)
    ckv = rmsnorm(proj[..., Q_LORA:Q_LORA + KV_LORA], g_kv)
    kpe = rope(proj[..., Q_LORA + KV_LORA:][:, :, None, :], pos)[:, :, 0, :]
    q = (cq @ w_q_up).reshape(B, T, H_C, NOPE_DIM + ROPE_DIM)
    q_nope = q[..., :NOPE_DIM]
    q_pe = rope(q[..., NOPE_DIM:], pos)
    w_kv = w_kv_up.reshape(KV_LORA, H_C, NOPE_DIM + V_C)
    w_uk, w_uv = w_kv[..., :NOPE_DIM], w_kv[..., NOPE_DIM:]
    if latent_past is None:
        k_nope = jnp.einsum('bsc,chd->bshd', ckv, w_uk)
        v = jnp.einsum('bsc,chd->bshd', ckv, w_uv)
        k = jnp.concatenate([k_nope, jnp.broadcast_to(kpe[:, :, None, :], (B, T, H_C, ROPE_DIM))], axis=-1)
        qf = jnp.concatenate([q_nope, q_pe], axis=-1)

        def attend(qb, pb):
            s = jnp.einsum('bthd,bshd->bhts', qb, k).astype(jnp.float32) * MLA_SCALE
            s = jnp.where(pos[None, :] <= pb[:, None], s, -jnp.inf)
            p = jax.nn.softmax(s, axis=-1).astype(v.dtype)
            return jnp.einsum('bhts,bshd->bthd', p, v)
    else:
        ckv_past, kpe_past = latent_past
        lat = jnp.concatenate([ckv_past, ckv], axis=1)
        kf = jnp.concatenate([lat, jnp.concatenate([kpe_past, kpe], axis=1)], axis=-1)
        k_pos = jnp.concatenate([jnp.arange(ckv_past.shape[1], dtype=jnp.int32), pos])
        qf = jnp.concatenate([jnp.einsum('bthd,chd->bthc', q_nope, w_uk), q_pe], axis=-1)

        def attend(qb, pb):
            s = jnp.einsum('bthc,bsc->bhts', qb, kf).astype(jnp.float32) * MLA_SCALE
            s = jnp.where(k_pos[None, :] <= pb[:, None], s, -jnp.inf)
            p = jax.nn.softmax(s, axis=-1).astype(lat.dtype)
            o_lat = jnp.einsum('bhts,bsc->bthc', p, lat)
            return jnp.einsum('bthc,chd->bthd', o_lat, w_uv)
    o = sweep_query_blocks(attend, qf, pos)
    return o.reshape(B, T, MIX_C) @ w_out, ckv, kpe


def run_trunk(x, ple, pos, W, past):
    B, T, _ = x.shape
    new = {name: [] for name in ('sb_k', 'sb_v', 'c', 'n', 'm', 'ckv', 'kpe')}
    h = x
    for i in range(DEPTH):
        hn = rmsnorm(h, W['norm_mix'][i])
        if i % 2 == 0:
            e = i // 2
            if past is None:
                state0 = (jnp.zeros((B, H_A, DQK_A, DV_A), jnp.float32),
                          jnp.zeros((B, H_A, DQK_A), jnp.float32),
                          jnp.zeros((B, H_A), jnp.float32))
                kv_past = None
            else:
                state0 = (past['c'][e].astype(jnp.float32), past['n'][e].astype(jnp.float32),
                          past['m'][e].astype(jnp.float32))
                kv_past = (gather_pages(past['sb_k'], e, past['page_table']),
                           gather_pages(past['sb_v'], e, past['page_table']))
            mix, (c, n, m), kb, vb = mlstm_sb_mixer(hn, W['w_in_ab'][e], W['b_gates_ab'][e], W['g_mlstm_head'][e],
                                                    W['w_out_ab'][e], state0, kv_past, pos)
            new['sb_k'].append(kb)
            new['sb_v'].append(vb)
            new['c'].append(c.astype(x.dtype))
            new['n'].append(n.astype(x.dtype))
            new['m'].append(m.astype(x.dtype))
        else:
            o = i // 2
            latent_past = None if past is None else (gather_pages(past['ckv'], o, past['page_table']),
                                                     gather_pages(past['kpe'], o, past['page_table']))
            mix, ckv, kpe = mla_mixer(hn, W['w_in_mla'][o], W['g_q_lora'][o], W['g_kv_lora'][o], W['w_q_up'][o],
                                      W['w_kv_up'][o], W['w_out_mla'][o], latent_past, pos)
            new['ckv'].append(ckv)
            new['kpe'].append(kpe)
        h = h + mix
        hf = rmsnorm(h, W['norm_ffn'][i])
        h = h + (jax.nn.silu(hf @ W['w_ffn_gate'][i]) * (hf @ W['w_ffn_up'][i])) @ W['w_ffn_down'][i]
        gate = jax.nn.sigmoid(rmsnorm(h, W['norm_ple'][i]) @ W['w_ple_gate'][i])
        h = h + gate * (ple[i].astype(h.dtype) @ W['w_ple_proj'][i])
    y = rmsnorm(h, W['norm_final'])
    return y, {name: jnp.stack(rows) for name, rows in new.items()}


def setup_inputs(seed: int = 0) -> dict:
    key = jax.random.key(seed)
    keys = jax.random.split(key, 40)
    counter = [0]

    def nxt():
        counter[0] += 1
        return keys[counter[0] - 1]

    def nrm(shape, scale=1.0):
        return jax.random.normal(nxt(), shape, jnp.float32) * scale

    def gain(shape):
        return 1.0 + nrm(shape, 0.02)

    n_pages = PAST_LEN // PAGE_SIZE
    n_used = DEC_BATCH * n_pages
    n_pool = n_used + max(1, n_used // 4)
    inputs = {}
    inputs['x_prompt'] = nrm((BATCH, SEQ, D_MODEL))
    inputs['x_sample'] = nrm((DEC_BATCH, DEC_SEQ, D_MODEL))
    inputs['p_prompt'] = nrm((DEPTH, BATCH, SEQ, PLE_DIM))
    inputs['p_sample'] = nrm((DEPTH, DEC_BATCH, DEC_SEQ, PLE_DIM))
    inputs['cache_sb_k'] = nrm((N_EVEN, n_pool, PAGE_SIZE, H_B, D_B))
    inputs['cache_sb_v'] = nrm((N_EVEN, n_pool, PAGE_SIZE, H_B, D_B))
    inputs['cache_mla_ckv'] = nrm((N_ODD, n_pool, PAGE_SIZE, KV_LORA))
    inputs['cache_mla_kpe'] = nrm((N_ODD, n_pool, PAGE_SIZE, ROPE_DIM))
    inputs['state_mlstm_c'] = nrm((N_EVEN, DEC_BATCH, H_A, DQK_A, DV_A))
    inputs['state_mlstm_n'] = nrm((N_EVEN, DEC_BATCH, H_A, DQK_A))
    inputs['state_mlstm_m'] = nrm((N_EVEN, DEC_BATCH, H_A))
    inputs['page_table'] = jax.random.permutation(nxt(), n_pool)[:n_used].reshape(DEC_BATCH, n_pages).astype(jnp.int32)
    inputs['norm_mix'] = gain((DEPTH, D_MODEL))
    inputs['norm_ffn'] = gain((DEPTH, D_MODEL))
    inputs['norm_ple'] = gain((DEPTH, D_MODEL))
    inputs['norm_final'] = gain((D_MODEL,))
    inputs['w_in_ab'] = nrm((N_EVEN, D_MODEL, IN_AB), D_MODEL ** -0.5)
    inputs['b_gates_ab'] = jnp.concatenate([nrm((N_EVEN, H_A), 0.1), FORGET_BIAS + nrm((N_EVEN, H_A), 0.1)], axis=-1)
    inputs['g_mlstm_head'] = gain((N_EVEN, H_A * DV_A))
    inputs['w_out_ab'] = nrm((N_EVEN, MIX_AB, D_MODEL), MIX_AB ** -0.5)
    inputs['w_in_mla'] = nrm((N_ODD, D_MODEL, IN_C), D_MODEL ** -0.5)
    inputs['g_q_lora'] = gain((N_ODD, Q_LORA))
    inputs['g_kv_lora'] = gain((N_ODD, KV_LORA))
    inputs['w_q_up'] = nrm((N_ODD, Q_LORA, H_C * (NOPE_DIM + ROPE_DIM)), Q_LORA ** -0.5)
    inputs['w_kv_up'] = nrm((N_ODD, KV_LORA, H_C * (NOPE_DIM + V_C)), KV_LORA ** -0.5)
    inputs['w_out_mla'] = nrm((N_ODD, MIX_C, D_MODEL), MIX_C ** -0.5)
    inputs['w_ffn_gate'] = nrm((DEPTH, D_MODEL, FFN_HIDDEN), D_MODEL ** -0.5)
    inputs['w_ffn_up'] = nrm((DEPTH, D_MODEL, FFN_HIDDEN), D_MODEL ** -0.5)
    inputs['w_ffn_down'] = nrm((DEPTH, FFN_HIDDEN, D_MODEL), FFN_HIDDEN ** -0.5)
    inputs['w_ple_gate'] = nrm((DEPTH, D_MODEL, D_MODEL), D_MODEL ** -0.5)
    inputs['w_ple_proj'] = nrm((DEPTH, PLE_DIM, D_MODEL), PLE_DIM ** -0.5)
    return inputs


def reference(x_prompt, x_sample, p_prompt, p_sample, cache_sb_k, cache_sb_v, cache_mla_ckv, cache_mla_kpe,
              state_mlstm_c, state_mlstm_n, state_mlstm_m, page_table, norm_mix, norm_ffn, norm_ple, norm_final,
              w_in_ab, b_gates_ab, g_mlstm_head, w_out_ab, w_in_mla, g_q_lora, g_kv_lora, w_q_up, w_kv_up,
              w_out_mla, w_ffn_gate, w_ffn_up, w_ffn_down, w_ple_gate, w_ple_proj):
    W = {'norm_mix': norm_mix, 'norm_ffn': norm_ffn, 'norm_ple': norm_ple, 'norm_final': norm_final,
         'w_in_ab': w_in_ab, 'b_gates_ab': b_gates_ab, 'g_mlstm_head': g_mlstm_head, 'w_out_ab': w_out_ab,
         'w_in_mla': w_in_mla, 'g_q_lora': g_q_lora, 'g_kv_lora': g_kv_lora, 'w_q_up': w_q_up,
         'w_kv_up': w_kv_up, 'w_out_mla': w_out_mla, 'w_ffn_gate': w_ffn_gate, 'w_ffn_up': w_ffn_up,
         'w_ffn_down': w_ffn_down, 'w_ple_gate': w_ple_gate, 'w_ple_proj': w_ple_proj}
    pos_prompt = jnp.arange(x_prompt.shape[1], dtype=jnp.int32)
    y_prompt, sp = run_trunk(x_prompt, p_prompt, pos_prompt, W, None)
    past_len = page_table.shape[1] * cache_sb_k.shape[2]
    past = {'sb_k': cache_sb_k, 'sb_v': cache_sb_v, 'ckv': cache_mla_ckv, 'kpe': cache_mla_kpe,
            'c': state_mlstm_c, 'n': state_mlstm_n, 'm': state_mlstm_m, 'page_table': page_table}
    pos_sample = past_len + jnp.arange(x_sample.shape[1], dtype=jnp.int32)
    y_sample, ss = run_trunk(x_sample, p_sample, pos_sample, W, past)
    return (y_prompt, y_sample,
            sp['sb_k'], sp['sb_v'], sp['c'], sp['n'], sp['m'], sp['ckv'], sp['kpe'],
            ss['sb_k'], ss['sb_v'], ss['c'], ss['n'], ss['m'], ss['ckv'], ss['kpe'])
```

```python
import functools

import jax
import jax.numpy as jnp
from jax import lax
from jax.experimental import pallas as pl
from jax.experimental.pallas import tpu as pltpu

F32 = jnp.float32
BF16 = jnp.bfloat16

EPS = 1e-6
H_A, DQK_A, DV_A, MLSTM_CHUNK = 4, 128, 256, 64
H_B, D_B = 8, 128
H_C, Q_LORA, KV_LORA, NOPE_DIM, ROPE_DIM, V_C = 16, 512, 512, 128, 64, 128
ROPE_THETA = 10000.0
SB_SCALE = D_B ** -0.5
MLA_SCALE = (NOPE_DIM + ROPE_DIM) ** -0.5
LANES = 128

AB_QA, AB_KA, AB_VA, AB_OA = 0, 512, 1024, 2048
AB_QB, AB_KB, AB_VB, AB_GATES = 3072, 4096, 5120, 6144
AB_WIDTH = AB_GATES + H_A * LANES
C_CQ, C_CKV, C_KPE, C_KROT = 0, 512, 1024, 1152
C_WIDTH = 1280
QU_NOPE, QU_PE, QU_ROT = 0, H_C * NOPE_DIM, H_C * NOPE_DIM + H_C * ROPE_DIM
QU_WIDTH = QU_ROT + H_C * ROPE_DIM
Q_WIDTH = QU_ROT

VMEM_LIMIT = 56 * 1024 * 1024


def _cparams(*sem):
    return pltpu.CompilerParams(dimension_semantics=sem, vmem_limit_bytes=VMEM_LIMIT)


def _dot(a, b):
    return jnp.dot(a, b, preferred_element_type=F32)


def _dot_nt(a, b):
    return lax.dot_general(a, b, (((1,), (1,)), ((), ())), preferred_element_type=F32)


def _dot_tn(a, b):
    return lax.dot_general(a, b, (((0,), (0,)), ((), ())), preferred_element_type=F32)


def _rms(x, g):
    return x * lax.rsqrt(jnp.mean(x * x, axis=-1, keepdims=True) + EPS) * g


def _softplus_neg_abs(z):
    return jnp.log(1.0 + jnp.exp(-jnp.abs(z)))


def _log_sigmoid(z):
    return jnp.minimum(z, 0.0) - _softplus_neg_abs(z)


def _sigmoid(z):
    return 1.0 / (1.0 + jnp.exp(-z))


def _tile(n, pref, align=8):
    if n <= pref:
        return n
    for t in range(pref - pref % align, 0, -align):
        if n % t == 0:
            return t
    raise ValueError((n, pref, align))


def _norm_matmul_body(x_ref, g_ref, w_ref, o_ref, xn_ref):
    @pl.when(pl.program_id(1) == 0)
    def _():
        xn_ref[...] = _rms(x_ref[...], g_ref[...]).astype(BF16)

    o_ref[...] = _dot(xn_ref[...], w_ref[...]).astype(o_ref.dtype)


def _norm_matmul(x, g, w, out_dtype, tm=512, tn=640):
    m, k = x.shape
    n = w.shape[1]
    tm, tn = _tile(m, tm), _tile(n, tn, LANES)
    return pl.pallas_call(
        _norm_matmul_body,
        grid=(m // tm, n // tn),
        in_specs=[pl.BlockSpec((tm, k), lambda i, j: (i, 0)),
                  pl.BlockSpec((1, k), lambda i, j: (0, 0)),
                  pl.BlockSpec((k, tn), lambda i, j: (0, j))],
        out_specs=pl.BlockSpec((tm, tn), lambda i, j: (i, j)),
        out_shape=jax.ShapeDtypeStruct((m, n), out_dtype),
        scratch_shapes=[pltpu.VMEM((tm, k), BF16)],
        compiler_params=_cparams("parallel", "arbitrary"),
        name="norm_matmul",
    )(x, g.reshape(1, k), w)


def _resid_matmul_body(*refs, n_in):
    h_ref, o_ref = refs[0], refs[-1]
    acc = h_ref[...]
    for i in range(n_in):
        acc = acc + _dot(refs[1 + 2 * i][...].astype(BF16), refs[2 + 2 * i][...])
    o_ref[...] = acc


def _resid_matmul(h, xs, ws, tm=512, tn=512):
    m, n = h.shape
    tm, tn = _tile(m, tm), _tile(n, tn, LANES)
    in_specs = [pl.BlockSpec((tm, tn), lambda i, j: (i, j))]
    args = [h]
    for x, w in zip(xs, ws):
        k = x.shape[1]
        in_specs += [pl.BlockSpec((tm, k), lambda i, j: (i, 0)),
                     pl.BlockSpec((k, tn), lambda i, j: (0, j))]
        args += [x, w]
    return pl.pallas_call(
        functools.partial(_resid_matmul_body, n_in=len(xs)),
        grid=(m // tm, n // tn),
        in_specs=in_specs,
        out_specs=pl.BlockSpec((tm, tn), lambda i, j: (i, j)),
        out_shape=jax.ShapeDtypeStruct((m, n), F32),
        compiler_params=_cparams("parallel", "arbitrary"),
        name="resid_matmul",
    )(*args)


def _ffn_body(h_ref, g_ref, wg_ref, wu_ref, wd_ref, o_ref, xn_ref):
    @pl.when(pl.program_id(1) == 0)
    def _():
        h = h_ref[...]
        xn_ref[...] = _rms(h, g_ref[...]).astype(BF16)
        o_ref[...] = h

    xn = xn_ref[...]
    a = _dot(xn, wg_ref[...])
    u = _dot(xn, wu_ref[...])
    act = (a * _sigmoid(a) * u).astype(BF16)
    o_ref[...] += _dot(act, wd_ref[...])


def _ffn(h, g, wg, wu, wd, tm=512, tf=512):
    m, d = h.shape
    f = wg.shape[1]
    tm, tf = _tile(m, tm), _tile(f, tf, LANES)
    return pl.pallas_call(
        _ffn_body,
        grid=(m // tm, f // tf),
        in_specs=[pl.BlockSpec((tm, d), lambda i, j: (i, 0)),
                  pl.BlockSpec((1, d), lambda i, j: (0, 0)),
                  pl.BlockSpec((d, tf), lambda i, j: (0, j)),
                  pl.BlockSpec((d, tf), lambda i, j: (0, j)),
                  pl.BlockSpec((tf, d), lambda i, j: (j, 0))],
        out_specs=pl.BlockSpec((tm, d), lambda i, j: (i, 0)),
        out_shape=jax.ShapeDtypeStruct((m, d), F32),
        scratch_shapes=[pltpu.VMEM((tm, d), BF16)],
        compiler_params=_cparams("parallel", "arbitrary"),
        name="ffn",
    )(h, g.reshape(1, d), wg, wu, wd)


def _ple_body(h_ref, p_ref, g_ref, wg_ref, wp_ref, gf_ref, o_ref, *, final):
    h = h_ref[...]
    xn = _rms(h, g_ref[...]).astype(BF16)
    gate = _sigmoid(_dot(xn, wg_ref[...]))
    out = h + gate * _dot(p_ref[...].astype(BF16), wp_ref[...])
    if final:
        out = _rms(out, gf_ref[...])
    o_ref[...] = out


def _ple(h, p, g, wg, wp, g_final, final, tm=256):
    m, d = h.shape
    pd = p.shape[1]
    tm = _tile(m, tm)
    return pl.pallas_call(
        functools.partial(_ple_body, final=final),
        grid=(m // tm,),
        in_specs=[pl.BlockSpec((tm, d), lambda i: (i, 0)),
                  pl.BlockSpec((tm, pd), lambda i: (i, 0)),
                  pl.BlockSpec((1, d), lambda i: (0, 0)),
                  pl.BlockSpec((d, d), lambda i: (0, 0)),
                  pl.BlockSpec((pd, d), lambda i: (0, 0)),
                  pl.BlockSpec((1, d), lambda i: (0, 0))],
        out_specs=pl.BlockSpec((tm, d), lambda i: (i, 0)),
        out_shape=jax.ShapeDtypeStruct((m, d), F32),
        compiler_params=_cparams("parallel"),
        name="ple",
    )(h, p, g.reshape(1, d), wg, wp, g_final.reshape(1, d))


def _mlstm_prompt_body(bias_ref, q_ref, k_ref, v_ref, oa_ref, gc_ref, gr_ref, gh_ref,
                       ha_ref, c_ref, n_ref, m_ref, cs, ns, ms, *, tb, chunk):
    h = pl.program_id(1)
    t = pl.program_id(2)

    @pl.when(t == 0)
    def _():
        cs[...] = jnp.zeros_like(cs)
        ns[...] = jnp.zeros_like(ns)
        ms[...] = jnp.zeros_like(ms)

    b_ig = bias_ref[h]
    b_fg = bias_ref[H_A + h]
    lane = lax.broadcasted_iota(jnp.int32, (tb, LANES), 1)
    gcb = gc_ref[...] + jnp.where(lane == 0, b_ig, b_fg)
    lsg_c = _log_sigmoid(gcb)
    row = lax.broadcasted_iota(jnp.int32, (8, tb), 0)
    grb = gr_ref[...] + jnp.where(row == 0, b_ig, b_fg)
    lsg_r = _log_sigmoid(grb)
    r = lax.broadcasted_iota(jnp.int32, (tb, tb), 0)
    c = lax.broadcasted_iota(jnp.int32, (tb, tb), 1)
    shift = chunk.bit_length() - 1
    same = jnp.right_shift(r, shift) == jnp.right_shift(c, shift)
    tril = jnp.where(same, jnp.where(r >= c, 1.0, 0.0), 0.0).astype(F32)
    bc_all = jnp.dot(tril, lsg_c, precision=lax.Precision.HIGHEST, preferred_element_type=F32)
    br_all = lax.dot_general(lsg_r, tril, (((1,), (1,)), ((), ())),
                             precision=lax.Precision.HIGHEST, preferred_element_type=F32)
    tt = lax.broadcasted_iota(jnp.int32, (chunk, chunk), 0)
    ss = lax.broadcasted_iota(jnp.int32, (chunk, chunk), 1)
    causal = ss <= tt
    gh = gh_ref[...]

    for ci in range(tb // chunk):
        lo, hi = ci * chunk, (ci + 1) * chunk
        bcol = bc_all[lo:hi, 1:2]
        igcol = gcb[lo:hi, 0:1]
        brow = br_all[1:2, lo:hi]
        igrow = grb[0:1, lo:hi]
        m_prev = ms[...]
        d = jnp.where(causal, bcol - brow + igrow, -jnp.inf)
        g = bcol + m_prev
        m_t = jnp.maximum(g, jnp.max(d, axis=-1, keepdims=True))
        w_in = jnp.exp(g - m_t)
        qf = q_ref[lo:hi, :]
        qb = qf.astype(BF16)
        kf = k_ref[lo:hi, :] * (DQK_A ** -0.5)
        kb = kf.astype(BF16)
        vb = v_ref[lo:hi, :].astype(BF16)
        s = _dot_nt(qb, kb) * jnp.exp(d - m_t)
        cmat = cs[...]
        nvec = ns[...]
        num = w_in * _dot(qb, cmat.astype(BF16)) + _dot(s.astype(BF16), vb)
        den = w_in * jnp.sum(qf * nvec, axis=-1, keepdims=True) + jnp.sum(s, axis=-1, keepdims=True)
        hh = num / jnp.maximum(jnp.abs(den), jnp.exp(-m_t))
        m_new = m_t[chunk - 1:chunk, :]
        decay = jnp.exp(g[chunk - 1:chunk, :] - m_new)
        wk = jnp.exp(bcol[chunk - 1:chunk, :] - bcol + igcol - m_new)
        kw = kf * wk
        cs[...] = decay * cmat + _dot_tn(kw.astype(BF16), vb)
        ns[...] = decay * nvec + jnp.sum(kw, axis=0, keepdims=True)
        ms[...] = m_new
        hn = _rms(hh, gh)
        ha_ref[lo:hi, :] = (hn * _sigmoid(oa_ref[lo:hi, :])).astype(ha_ref.dtype)

    @pl.when(t == pl.num_programs(2) - 1)
    def _():
        c_ref[...] = cs[...]
        n_ref[...] = ns[...]
        m_ref[...] = ms[...]


def _mlstm_prompt(proj, gates_t, b_gates, g_head, batch, seq):
    chunk = MLSTM_CHUNK if seq % MLSTM_CHUNK == 0 else seq
    tb = _tile(seq, 512) if seq % MLSTM_CHUNK == 0 else seq
    nt = seq // tb
    m = batch * seq
    qo, ko, vo, oo, go = (AB_QA // DQK_A, AB_KA // DQK_A, AB_VA // DV_A, AB_OA // DV_A, AB_GATES // LANES)
    return pl.pallas_call(
        functools.partial(_mlstm_prompt_body, tb=tb, chunk=chunk),
        grid=(batch, H_A, nt),
        in_specs=[pl.BlockSpec(memory_space=pltpu.SMEM),
                  pl.BlockSpec((tb, DQK_A), lambda b, h, t: (b * nt + t, qo + h)),
                  pl.BlockSpec((tb, DQK_A), lambda b, h, t: (b * nt + t, ko + h)),
                  pl.BlockSpec((tb, DV_A), lambda b, h, t: (b * nt + t, vo + h)),
                  pl.BlockSpec((tb, DV_A), lambda b, h, t: (b * nt + t, oo + h)),
                  pl.BlockSpec((tb, LANES), lambda b, h, t: (b * nt + t, go + h)),
                  pl.BlockSpec((None, 8, tb), lambda b, h, t: (h, 0, b * nt + t)),
                  pl.BlockSpec((1, DV_A), lambda b, h, t: (0, h))],
        out_specs=[pl.BlockSpec((tb, DV_A), lambda b, h, t: (b * nt + t, h)),
                   pl.BlockSpec((None, None, DQK_A, DV_A), lambda b, h, t: (b, h, 0, 0)),
                   pl.BlockSpec((None, None, 1, DQK_A), lambda b, h, t: (b, h, 0, 0)),
                   pl.BlockSpec((None, None, 1, 1), lambda b, h, t: (b, h, 0, 0))],
        out_shape=[jax.ShapeDtypeStruct((m, H_A * DV_A), BF16),
                   jax.ShapeDtypeStruct((batch, H_A, DQK_A, DV_A), F32),
                   jax.ShapeDtypeStruct((batch, H_A, 1, DQK_A), F32),
                   jax.ShapeDtypeStruct((batch, H_A, 1, 1), F32)],
        scratch_shapes=[pltpu.VMEM((DQK_A, DV_A), F32), pltpu.VMEM((1, DQK_A), F32), pltpu.VMEM((1, 1), F32)],
        compiler_params=_cparams("parallel", "parallel", "arbitrary"),
        name="mlstm_prompt",
    )(b_gates, proj, proj, proj, proj, proj, gates_t, g_head.reshape(1, H_A * DV_A))


def _mlstm_decode_body(bias_ref, q_ref, k_ref, v_ref, oa_ref, gc_ref, gh_ref, c0_ref, n0_ref, m0_ref,
                       ha_ref, c_ref, n_ref, m_ref, *, tb):
    h = pl.program_id(1)
    b_ig = bias_ref[h]
    b_fg = bias_ref[H_A + h]
    gc = gc_ref[...]
    ig = gc[:, 0:1] + b_ig
    fg = _log_sigmoid(gc[:, 1:2] + b_fg)
    m0 = m0_ref[...]
    g = fg + m0
    m_t = jnp.maximum(g, ig)
    w_in = jnp.exp(g - m_t)
    e_ig = jnp.exp(ig - m_t)
    qf = q_ref[...]
    kf = k_ref[...] * (DQK_A ** -0.5)
    vf = v_ref[...]
    vb = vf.astype(BF16)
    qk = jnp.sum(qf.astype(BF16).astype(F32) * kf.astype(BF16).astype(F32), axis=-1, keepdims=True)
    s = qk * e_ig
    kw = kf * e_ig
    rows = lax.broadcasted_iota(jnp.int32, (tb, 1), 0)

    def step(b, hq):
        sel = rows == b
        cb = c0_ref[b]
        hq = hq + _dot(jnp.where(sel, qf, 0.0).astype(BF16), cb.astype(BF16))
        upd = _dot_tn(jnp.where(sel, kw, 0.0).astype(BF16), vb)
        decay = jnp.sum(jnp.where(sel, w_in, 0.0), axis=0, keepdims=True)
        c_ref[b] = decay * cb + upd
        return hq

    hq = lax.fori_loop(0, tb, step, jnp.zeros((tb, DV_A), F32))
    n0 = n0_ref[...]
    num = w_in * hq + s * vb.astype(F32)
    den = w_in * jnp.sum(qf * n0, axis=-1, keepdims=True) + s
    hh = num / jnp.maximum(jnp.abs(den), jnp.exp(-m_t))
    n_ref[...] = w_in * n0 + kw
    m_ref[...] = m_t
    hn = _rms(hh, gh_ref[...])
    ha_ref[...] = (hn * _sigmoid(oa_ref[...])).astype(ha_ref.dtype)


def _mlstm_decode(proj, b_gates, g_head, c0, n0, m0, layer):
    batch = proj.shape[0]
    tb = _tile(batch, 32)
    qo, ko, vo, oo, go = (AB_QA // DQK_A, AB_KA // DQK_A, AB_VA // DV_A, AB_OA // DV_A, AB_GATES // LANES)
    return pl.pallas_call(
        functools.partial(_mlstm_decode_body, tb=tb),
        grid=(batch // tb, H_A),
        in_specs=[pl.BlockSpec(memory_space=pltpu.SMEM),
                  pl.BlockSpec((tb, DQK_A), lambda i, h: (i, qo + h)),
                  pl.BlockSpec((tb, DQK_A), lambda i, h: (i, ko + h)),
                  pl.BlockSpec((tb, DV_A), lambda i, h: (i, vo + h)),
                  pl.BlockSpec((tb, DV_A), lambda i, h: (i, oo + h)),
                  pl.BlockSpec((tb, LANES), lambda i, h: (i, go + h)),
                  pl.BlockSpec((1, DV_A), lambda i, h: (0, h)),
                  pl.BlockSpec((None, tb, None, DQK_A, DV_A), lambda i, h: (layer, i, h, 0, 0)),
                  pl.BlockSpec((tb, DQK_A), lambda i, h: (i, h)),
                  pl.BlockSpec((None, tb, 1), lambda i, h: (h, i, 0))],
        out_specs=[pl.BlockSpec((tb, DV_A), lambda i, h: (i, h)),
                   pl.BlockSpec((tb, None, DQK_A, DV_A), lambda i, h: (i, h, 0, 0)),
                   pl.BlockSpec((tb, DQK_A), lambda i, h: (i, h)),
                   pl.BlockSpec((None, tb, 1), lambda i, h: (h, i, 0))],
        out_shape=[jax.ShapeDtypeStruct((batch, H_A * DV_A), BF16),
                   jax.ShapeDtypeStruct((batch, H_A, DQK_A, DV_A), F32),
                   jax.ShapeDtypeStruct((batch, H_A * DQK_A), F32),
                   jax.ShapeDtypeStruct((H_A, batch, 1), F32)],
        compiler_params=_cparams("parallel", "parallel"),
        name="mlstm_decode",
    )(b_gates, proj, proj, proj, proj, proj, g_head.reshape(1, H_A * DV_A), c0, n0, m0)


def _sb_weights(z, mask, tri, carry):
    sp = _softplus_neg_abs(z)
    ls = jnp.minimum(z, 0.0) - sp
    lk = -(jnp.maximum(z, 0.0) + sp)
    if mask is not None:
        lk = jnp.where(mask, lk, 0.0)
    hi = lk.astype(BF16)
    lo = (lk - hi.astype(F32)).astype(BF16)
    la = _dot(hi, tri) + _dot(lo, tri) + carry
    a = jnp.exp(ls + la)
    if mask is not None:
        a = jnp.where(mask, a, 0.0)
    return a, carry + jnp.sum(lk, axis=-1, keepdims=True)


def _later_key_matrix(n):
    r = lax.broadcasted_iota(jnp.int32, (n, n), 0)
    c = lax.broadcasted_iota(jnp.int32, (n, n), 1)
    return jnp.where(r > c, 1.0, 0.0).astype(BF16)


def _sb_prompt_body(q_ref, k_ref, v_ref, o_ref, acc_ref, carry_ref, *, tq):
    i = pl.program_id(1)
    jj = pl.program_id(2)
    j = i - jj

    @pl.when(jj == 0)
    def _():
        acc_ref[...] = jnp.zeros_like(acc_ref)
        carry_ref[...] = jnp.zeros_like(carry_ref)

    @pl.when(j >= 0)
    def _():
        qpos = i * tq + lax.broadcasted_iota(jnp.int32, (tq, tq), 0)
        kpos = j * tq + lax.broadcasted_iota(jnp.int32, (tq, tq), 1)
        mask = kpos < qpos
        tri = _later_key_matrix(tq)
        for h in range(H_B):
            sl = slice(h * D_B, (h + 1) * D_B)
            z = _dot_nt(q_ref[:, sl].astype(BF16), k_ref[:, sl].astype(BF16)) * SB_SCALE
            a, carry = _sb_weights(z, mask, tri, carry_ref[h])
            acc_ref[h] += _dot(a.astype(BF16), v_ref[:, sl].astype(BF16))
            carry_ref[h] = carry

    @pl.when(jj == pl.num_programs(2) - 1)
    def _():
        for h in range(H_B):
            o_ref[:, h * D_B:(h + 1) * D_B] = acc_ref[h].astype(o_ref.dtype)


def _sb_prompt(proj, batch, seq):
    tq = _tile(seq, 256)
    nq = seq // tq
    width = H_B * D_B
    qo, ko, vo = AB_QB // width, AB_KB // width, AB_VB // width
    return pl.pallas_call(
        functools.partial(_sb_prompt_body, tq=tq),
        grid=(batch, nq, nq),
        in_specs=[pl.BlockSpec((tq, width), lambda b, i, jj: (b * nq + i, qo)),
                  pl.BlockSpec((tq, width), lambda b, i, jj: (b * nq + jnp.maximum(i - jj, 0), ko)),
                  pl.BlockSpec((tq, width), lambda b, i, jj: (b * nq + jnp.maximum(i - jj, 0), vo))],
        out_specs=pl.BlockSpec((tq, width), lambda b, i, jj: (b * nq + i, 0)),
        out_shape=jax.ShapeDtypeStruct((batch * seq, width), BF16),
        scratch_shapes=[pltpu.VMEM((H_B, tq, D_B), F32), pltpu.VMEM((H_B, tq, 1), F32)],
        compiler_params=_cparams("parallel", "parallel", "arbitrary"),
        name="sb_prompt",
    )(proj, proj, proj)


def _sb_decode_body(pt_ref, q_ref, *refs, pages_per_step, page):
    del pt_ref
    g_n = pages_per_step
    k_refs, v_refs = refs[:g_n], refs[g_n:2 * g_n]
    o_ref, acc_ref, carry_ref = refs[2 * g_n:]
    jj = pl.program_id(1)

    @pl.when(jj == 0)
    def _():
        acc_ref[...] = jnp.zeros_like(acc_ref)
        carry_ref[...] = jnp.zeros_like(carry_ref)

    q = q_ref[...]
    qall = jnp.concatenate([q[:, h * D_B:(h + 1) * D_B] for h in range(H_B)], axis=0).astype(BF16)
    head = lax.broadcasted_iota(jnp.int32, (H_B, page), 0)
    head_d = lax.broadcasted_iota(jnp.int32, (H_B, D_B), 0)
    tri = _later_key_matrix(page)
    for g in range(g_n):
        z = jnp.zeros((H_B, page), F32)
        for h in range(H_B):
            z = jnp.where(head == h, _dot_nt(qall, k_refs[g][:, h, :].astype(BF16)), z)
        a, carry = _sb_weights(z * SB_SCALE, None, tri, carry_ref[...])
        ab = a.astype(BF16)
        upd = jnp.zeros((H_B, D_B), F32)
        for h in range(H_B):
            upd = jnp.where(head_d == h, _dot(ab, v_refs[g][:, h, :].astype(BF16)), upd)
        acc_ref[...] += upd
        carry_ref[...] = carry

    @pl.when(jj == pl.num_programs(1) - 1)
    def _():
        for h in range(H_B):
            o_ref[:, h * D_B:(h + 1) * D_B] = acc_ref[h:h + 1, :].astype(o_ref.dtype)


def _sb_decode(q, cache_k, cache_v, page_table, layer):
    batch = q.shape[0]
    n_pages = page_table.shape[1]
    page = cache_k.shape[2]
    g_n = 8 if n_pages % 8 == 0 else 1
    steps = n_pages // g_n

    def page_spec(g):
        return pl.BlockSpec((None, None, page, H_B, D_B),
                            lambda b, jj, pt: (layer, pt[b, n_pages - 1 - (jj * g_n + g)], 0, 0, 0))

    grid_spec = pltpu.PrefetchScalarGridSpec(
        num_scalar_prefetch=1,
        grid=(batch, steps),
        in_specs=[pl.BlockSpec((None, 1, H_B * D_B), lambda b, jj, pt: (b, 0, 0))]
        + [page_spec(g) for g in range(g_n)] * 2,
        out_specs=pl.BlockSpec((None, 1, H_B * D_B), lambda b, jj, pt: (b, 0, 0)),
        scratch_shapes=[pltpu.VMEM((H_B, D_B), F32), pltpu.VMEM((H_B, 1), F32)],
    )
    return pl.pallas_call(
        functools.partial(_sb_decode_body, pages_per_step=g_n, page=page),
        grid_spec=grid_spec,
        out_shape=jax.ShapeDtypeStruct((batch, 1, H_B * D_B), BF16),
        compiler_params=_cparams("parallel", "arbitrary"),
        name="sb_decode",
    )(page_table, q, *([cache_k] * g_n), *([cache_v] * g_n))


def _mla_up_body(pr_ref, gq_ref, gkv_ref, wq_ref, wkv_ref, cos_ref, sin_ref, *out_refs, with_kv):
    q_ref, ckv_ref, kpe_ref = out_refs[:3]
    cos = cos_ref[...]
    sin = sin_ref[...]
    cq = _rms(pr_ref[:, C_CQ:C_CQ + Q_LORA], gq_ref[...]).astype(BF16)
    ckv = _rms(pr_ref[:, C_CKV:C_CKV + KV_LORA], gkv_ref[...])
    ckv_ref[...] = ckv
    kpe_ref[...] = pr_ref[:, C_KPE:C_KPE + LANES] * cos + pr_ref[:, C_KROT:C_KROT + LANES] * sin
    qa = _dot(cq, wq_ref[...])
    q_ref[:, QU_NOPE:QU_PE] = qa[:, QU_NOPE:QU_PE].astype(BF16)
    for c in range((QU_ROT - QU_PE) // LANES):
        pe = qa[:, QU_PE + c * LANES:QU_PE + (c + 1) * LANES]
        rot = qa[:, QU_ROT + c * LANES:QU_ROT + (c + 1) * LANES]
        q_ref[:, QU_PE + c * LANES:QU_PE + (c + 1) * LANES] = (pe * cos + rot * sin).astype(BF16)
    if with_kv:
        out_refs[3][...] = _dot(ckv.astype(BF16), wkv_ref[...]).astype(BF16)


def _mla_up(proj, g_q, g_kv, w_q, w_kv, cos, sin, with_kv, tm=256):
    m = proj.shape[0]
    tm = _tile(m, tm)
    kv_w = w_kv.shape[1]
    out_specs = [pl.BlockSpec((tm, Q_WIDTH), lambda i: (i, 0)),
                 pl.BlockSpec((tm, KV_LORA), lambda i: (i, 0)),
                 pl.BlockSpec((tm, LANES), lambda i: (i, 0))]
    out_shape = [jax.ShapeDtypeStruct((m, Q_WIDTH), BF16),
                 jax.ShapeDtypeStruct((m, KV_LORA), F32),
                 jax.ShapeDtypeStruct((m, LANES), F32)]
    if with_kv:
        out_specs.append(pl.BlockSpec((tm, kv_w), lambda i: (i, 0)))
        out_shape.append(jax.ShapeDtypeStruct((m, kv_w), BF16))
    return pl.pallas_call(
        functools.partial(_mla_up_body, with_kv=with_kv),
        grid=(m // tm,),
        in_specs=[pl.BlockSpec((tm, C_WIDTH), lambda i: (i, 0)),
                  pl.BlockSpec((1, Q_LORA), lambda i: (0, 0)),
                  pl.BlockSpec((1, KV_LORA), lambda i: (0, 0)),
                  pl.BlockSpec((Q_LORA, QU_WIDTH), lambda i: (0, 0)),
                  pl.BlockSpec((KV_LORA, kv_w), lambda i: (0, 0)),
                  pl.BlockSpec((tm, LANES), lambda i: (i, 0)),
                  pl.BlockSpec((tm, LANES), lambda i: (i, 0))],
        out_specs=out_specs,
        out_shape=out_shape,
        compiler_params=_cparams("parallel"),
        name="mla_up",
    )(proj, g_q.reshape(1, Q_LORA), g_kv.reshape(1, KV_LORA), w_q, w_kv, cos, sin)


def _mla_prompt_body(q_ref, kn_ref, v_ref, kpe_ref, o_ref, m_ref, l_ref, acc_ref, *, tq):
    i = pl.program_id(1)
    j = pl.program_id(2)

    @pl.when(j == 0)
    def _():
        m_ref[...] = jnp.full_like(m_ref, -1e30)
        l_ref[...] = jnp.zeros_like(l_ref)
        acc_ref[...] = jnp.zeros_like(acc_ref)

    @pl.when(j <= i)
    def _():
        qpos = i * tq + lax.broadcasted_iota(jnp.int32, (tq, tq), 0)
        kpos = j * tq + lax.broadcasted_iota(jnp.int32, (tq, tq), 1)
        mask = kpos <= qpos
        kpe = kpe_ref[...]
        lane = lax.broadcasted_iota(jnp.int32, kpe.shape, 1)
        kpe_sel = (jnp.where(lane < ROPE_DIM, kpe, 0.0).astype(BF16),
                   jnp.where(lane >= ROPE_DIM, kpe, 0.0).astype(BF16))
        for h in range(H_C):
            sl = slice(h * NOPE_DIM, (h + 1) * NOPE_DIM)
            pe = slice(QU_PE + (h // 2) * LANES, QU_PE + (h // 2 + 1) * LANES)
            s = _dot_nt(q_ref[:, sl], kn_ref[:, sl]) + _dot_nt(q_ref[:, pe], kpe_sel[h % 2])
            s = jnp.where(mask, s * MLA_SCALE, -jnp.inf)
            m_prev = m_ref[h]
            m_new = jnp.maximum(m_prev, jnp.max(s, axis=-1, keepdims=True))
            alpha = jnp.exp(m_prev - m_new)
            p = jnp.exp(s - m_new)
            l_ref[h] = alpha * l_ref[h] + jnp.sum(p, axis=-1, keepdims=True)
            acc_ref[h] = alpha * acc_ref[h] + _dot(p.astype(BF16), v_ref[:, sl])
            m_ref[h] = m_new

    @pl.when(j == pl.num_programs(2) - 1)
    def _():
        for h in range(H_C):
            o_ref[:, h * V_C:(h + 1) * V_C] = (acc_ref[h] / l_ref[h]).astype(o_ref.dtype)


def _mla_prompt(q, kv, kpe2, batch, seq):
    tq = _tile(seq, 256)
    nq = seq // tq
    kw, vw = H_C * NOPE_DIM, H_C * V_C
    assert kw == vw
    return pl.pallas_call(
        functools.partial(_mla_prompt_body, tq=tq),
        grid=(batch, nq, nq),
        in_specs=[pl.BlockSpec((tq, Q_WIDTH), lambda b, i, j: (b * nq + i, 0)),
                  pl.BlockSpec((tq, kw), lambda b, i, j: (b * nq + jnp.minimum(i, j), 0)),
                  pl.BlockSpec((tq, vw), lambda b, i, j: (b * nq + jnp.minimum(i, j), 1)),
                  pl.BlockSpec((tq, LANES), lambda b, i, j: (b * nq + jnp.minimum(i, j), 0))],
        out_specs=pl.BlockSpec((tq, vw), lambda b, i, j: (b * nq + i, 0)),
        out_shape=jax.ShapeDtypeStruct((batch * seq, vw), BF16),
        scratch_shapes=[pltpu.VMEM((H_C, tq, 1), F32), pltpu.VMEM((H_C, tq, 1), F32),
                        pltpu.VMEM((H_C, tq, V_C), F32)],
        compiler_params=_cparams("parallel", "parallel", "arbitrary"),
        name="mla_prompt",
    )(q, kv, kv, kpe2)


def _head_matmul_body(x_ref, w_ref, o_ref):
    o_ref[...] = _dot(x_ref[...], w_ref[...]).astype(o_ref.dtype)


def _head_matmul(x, w, d_in, col0=0):
    m = x.shape[0]
    n_h, _, d_out = w.shape
    off = col0 // d_in
    return pl.pallas_call(
        _head_matmul_body,
        grid=(n_h,),
        in_specs=[pl.BlockSpec((m, d_in), lambda h: (0, off + h)),
                  pl.BlockSpec((None, d_in, d_out), lambda h: (h, 0, 0))],
        out_specs=pl.BlockSpec((m, d_out), lambda h: (0, h)),
        out_shape=jax.ShapeDtypeStruct((m, n_h * d_out), BF16),
        compiler_params=_cparams("parallel"),
        name="head_matmul",
    )(x, w)


MLA_KEY_WIDTH = 640


def _mla_decode_body(pt_ref, ql_ref, qp_ref, cn_ref, kn_ref, *refs, pages_per_step, page):
    del pt_ref
    g_n = pages_per_step
    c_refs, p_refs = refs[:g_n], refs[g_n:2 * g_n]
    o_ref, qbuf, kbuf, m_ref, l_ref, acc_ref = refs[2 * g_n:]
    jj = pl.program_id(1)

    @pl.when(jj == 0)
    def _():
        ql = ql_ref[...]
        qp = qp_ref[...]
        qbuf[:, :KV_LORA] = ql
        qbuf[:, KV_LORA:KV_LORA + ROPE_DIM] = qp
        qbuf[:, KV_LORA + ROPE_DIM:] = jnp.zeros((H_C, MLA_KEY_WIDTH - KV_LORA - ROPE_DIM), BF16)
        kbuf[:, KV_LORA + ROPE_DIM:] = jnp.zeros((g_n * page, MLA_KEY_WIDTH - KV_LORA - ROPE_DIM), BF16)
        lat = cn_ref[...].astype(BF16).astype(F32)
        kpe = kn_ref[...].astype(BF16).astype(F32)
        s0 = (jnp.sum(ql.astype(F32) * lat, axis=-1, keepdims=True)
              + jnp.sum(qp.astype(F32) * kpe, axis=-1, keepdims=True)) * MLA_SCALE
        m_ref[...] = s0
        l_ref[...] = jnp.ones_like(l_ref)
        acc_ref[...] = jnp.broadcast_to(lat, acc_ref.shape)

    for g in range(g_n):
        kbuf[g * page:(g + 1) * page, :KV_LORA] = c_refs[g][...].astype(BF16)
        kbuf[g * page:(g + 1) * page, KV_LORA:KV_LORA + ROPE_DIM] = p_refs[g][...].astype(BF16)
    s = _dot_nt(qbuf[...], kbuf[...]) * MLA_SCALE
    m_prev = m_ref[...]
    m_new = jnp.maximum(m_prev, jnp.max(s, axis=-1, keepdims=True))
    alpha = jnp.exp(m_prev - m_new)
    p = jnp.exp(s - m_new)
    l_ref[...] = alpha * l_ref[...] + jnp.sum(p, axis=-1, keepdims=True)
    acc_ref[...] = alpha * acc_ref[...] + _dot(p.astype(BF16), kbuf[:, :KV_LORA])
    m_ref[...] = m_new

    @pl.when(jj == pl.num_programs(1) - 1)
    def _():
        o_ref[...] = (acc_ref[...] / l_ref[...]).astype(o_ref.dtype)


def _mla_decode(q_lat, q_pe, ckv_new, kpe_new, cache_ckv, cache_kpe, page_table, layer):
    batch = q_lat.shape[0]
    n_pages = page_table.shape[1]
    page = cache_ckv.shape[2]
    g_n = 8 if n_pages % 8 == 0 else 1
    steps = n_pages // g_n

    def page_spec(width, g):
        return pl.BlockSpec((None, None, page, width), lambda b, jj, pt: (layer, pt[b, jj * g_n + g], 0, 0))

    grid_spec = pltpu.PrefetchScalarGridSpec(
        num_scalar_prefetch=1,
        grid=(batch, steps),
        in_specs=[pl.BlockSpec((None, H_C, KV_LORA), lambda b, jj, pt: (b, 0, 0)),
                  pl.BlockSpec((None, H_C, ROPE_DIM), lambda b, jj, pt: (b, 0, 0)),
                  pl.BlockSpec((None, 1, KV_LORA), lambda b, jj, pt: (b, 0, 0)),
                  pl.BlockSpec((None, 1, ROPE_DIM), lambda b, jj, pt: (b, 0, 0))]
        + [page_spec(KV_LORA, g) for g in range(g_n)] + [page_spec(ROPE_DIM, g) for g in range(g_n)],
        out_specs=pl.BlockSpec((None, H_C, KV_LORA), lambda b, jj, pt: (b, 0, 0)),
        scratch_shapes=[pltpu.VMEM((H_C, MLA_KEY_WIDTH), BF16),
                        pltpu.VMEM((g_n * page, MLA_KEY_WIDTH), BF16),
                        pltpu.VMEM((H_C, 1), F32), pltpu.VMEM((H_C, 1), F32),
                        pltpu.VMEM((H_C, KV_LORA), F32)],
    )
    return pl.pallas_call(
        functools.partial(_mla_decode_body, pages_per_step=g_n, page=page),
        grid_spec=grid_spec,
        out_shape=jax.ShapeDtypeStruct((batch, H_C, KV_LORA), BF16),
        compiler_params=_cparams("parallel", "arbitrary"),
        name="mla_decode",
    )(page_table, q_lat, q_pe, ckv_new, kpe_new, *([cache_ckv] * g_n), *([cache_kpe] * g_n))


def _rot_cols(w):
    k = w.shape[0]
    w4 = w.reshape(k, -1, 2, ROPE_DIM // 2)
    return jnp.stack([-w4[:, :, 1], w4[:, :, 0]], axis=2).reshape(k, -1)


def _prep_ab(w_in, w_out):
    d = w_in.shape[0]
    sizes = (H_A * DQK_A, H_A * DQK_A, H_A * DV_A, H_A * DV_A, 2 * H_A, H_B * D_B, H_B * D_B, H_B * D_B)
    parts, start = [], 0
    for sz in sizes:
        parts.append(w_in[:, start:start + sz])
        start += sz
    qa, ka, va, oa, gates, qb, kb, vb = parts
    gcols = jnp.zeros((d, H_A, LANES), w_in.dtype)
    gcols = gcols.at[:, :, 0].set(gates[:, :H_A]).at[:, :, 1].set(gates[:, H_A:])
    w_ext = jnp.concatenate([qa, ka, va, oa, qb, kb, vb, gcols.reshape(d, H_A * LANES)], axis=1).astype(BF16)
    w_out = w_out.astype(BF16)
    return w_ext, w_out[:H_A * DV_A], w_out[H_A * DV_A:]


def _prep_mla(w_in, w_q_up, w_kv_up):
    kpe_w = w_in[:, Q_LORA + KV_LORA:]
    rot_w = _rot_cols(kpe_w)
    w_in_ext = jnp.concatenate([w_in[:, :Q_LORA + KV_LORA], kpe_w, kpe_w, rot_w, rot_w], axis=1).astype(BF16)
    wq = w_q_up.reshape(Q_LORA, H_C, NOPE_DIM + ROPE_DIM)
    wq_pe = wq[:, :, NOPE_DIM:].reshape(Q_LORA, H_C * ROPE_DIM)
    w_q_ext = jnp.concatenate([wq[:, :, :NOPE_DIM].reshape(Q_LORA, H_C * NOPE_DIM), wq_pe, _rot_cols(wq_pe)],
                              axis=1).astype(BF16)
    wkv = w_kv_up.reshape(KV_LORA, H_C, NOPE_DIM + V_C)
    w_uk, w_uv = wkv[:, :, :NOPE_DIM], wkv[:, :, NOPE_DIM:]
    w_kv_ext = jnp.concatenate([w_uk.reshape(KV_LORA, -1), w_uv.reshape(KV_LORA, -1)], axis=1).astype(BF16)
    w_uk_t = jnp.transpose(w_uk, (1, 2, 0)).astype(BF16)
    w_uv_h = jnp.transpose(w_uv, (1, 0, 2)).astype(BF16)
    return w_in_ext, w_q_ext, w_kv_ext, w_uk_t, w_uv_h


def _rope_tables(pos):
    half = ROPE_DIM // 2
    freqs = ROPE_THETA ** (-jnp.arange(half, dtype=F32) / half)
    ang = pos.astype(F32)[:, None] * freqs[None, :]
    reps = LANES // half
    return jnp.tile(jnp.cos(ang), (1, reps)), jnp.tile(jnp.sin(ang), (1, reps))


def _trunk(x, ple, pos, wts, past):
    batch, seq, d = x.shape
    m = batch * seq
    h = x.reshape(m, d)
    depth = wts['norm_mix'].shape[0]
    cos, sin = _rope_tables(pos)
    cos, sin = jnp.tile(cos, (batch, 1)), jnp.tile(sin, (batch, 1))
    new = {name: [] for name in ('sb_k', 'sb_v', 'c', 'n', 'm', 'ckv', 'kpe')}
    for i in range(depth):
        if i % 2 == 0:
            e = i // 2
            w_ext, w_out_a, w_out_b = wts['ab'][e]
            proj = _norm_matmul(h, wts['norm_mix'][i], w_ext, F32)
            kb = proj[:, AB_KB:AB_KB + H_B * D_B]
            vb = proj[:, AB_VB:AB_VB + H_B * D_B]
            b_gates = wts['b_gates_ab'][e].astype(F32)
            g_head = wts['g_mlstm_head'][e].astype(F32)
            if past is None:
                gates_t = proj[:, AB_GATES:].reshape(m, H_A, LANES)[:, :, :8]
                gates_t = jnp.transpose(gates_t, (1, 2, 0))
                ha, c_new, n_new, m_new = _mlstm_prompt(proj, gates_t, b_gates, g_head, batch, seq)
                n_new = n_new.reshape(batch, H_A, DQK_A)
                m_new = m_new.reshape(batch, H_A)
                hb = _sb_prompt(proj, batch, seq)
            else:
                n0 = past['n'][e].astype(F32).reshape(batch, H_A * DQK_A)
                m0 = jnp.transpose(past['m'][e].astype(F32))[:, :, None]
                ha, c_new, n_new, m_new = _mlstm_decode(proj, b_gates, g_head, past['c'], n0, m0, e)
                n_new = n_new.reshape(batch, H_A, DQK_A)
                m_new = jnp.transpose(m_new[:, :, 0])
                qb = proj[:, AB_QB:AB_QB + H_B * D_B].reshape(batch, 1, H_B * D_B)
                hb = _sb_decode(qb, past['sb_k'], past['sb_v'], past['page_table'], e)
                hb = hb.reshape(batch, H_B * D_B)
            h = _resid_matmul(h, [ha, hb], [w_out_a, w_out_b])
            new['sb_k'].append(kb.reshape(batch, seq, H_B, D_B))
            new['sb_v'].append(vb.reshape(batch, seq, H_B, D_B))
            new['c'].append(c_new)
            new['n'].append(n_new)
            new['m'].append(m_new)
        else:
            o = i // 2
            w_in_ext, w_q_ext, w_kv_ext, w_uk_t, w_uv_h = wts['mla'][o]
            proj = _norm_matmul(h, wts['norm_mix'][i], w_in_ext, F32)
            g_q, g_kv = wts['g_q_lora'][o], wts['g_kv_lora'][o]
            if past is None:
                q, ckv, kpe2, kv = _mla_up(proj, g_q, g_kv, w_q_ext, w_kv_ext, cos, sin, True)
                att = _mla_prompt(q, kv, kpe2, batch, seq)
            else:
                q, ckv, kpe2 = _mla_up(proj, g_q, g_kv, w_q_ext, w_kv_ext, cos, sin, False)
                q_lat = _head_matmul(q, w_uk_t, NOPE_DIM).reshape(batch, H_C, KV_LORA)
                q_pe = q[:, QU_PE:].reshape(batch, H_C, ROPE_DIM)
                o_lat = _mla_decode(q_lat, q_pe, ckv.reshape(batch, 1, KV_LORA),
                                    kpe2[:, :ROPE_DIM].reshape(batch, 1, ROPE_DIM),
                                    past['ckv'], past['kpe'], past['page_table'], o)
                att = _head_matmul(o_lat.reshape(batch, H_C * KV_LORA), w_uv_h, KV_LORA)
            h = _resid_matmul(h, [att], [wts['w_out_mla'][o]])
            new['ckv'].append(ckv.reshape(batch, seq, KV_LORA))
            new['kpe'].append(kpe2[:, :ROPE_DIM].reshape(batch, seq, ROPE_DIM))
        h = _ffn(h, wts['norm_ffn'][i], wts['w_ffn_gate'][i], wts['w_ffn_up'][i], wts['w_ffn_down'][i])
        h = _ple(h, ple[i].reshape(m, -1), wts['norm_ple'][i], wts['w_ple_gate'][i], wts['w_ple_proj'][i],
                 wts['norm_final'], final=(i == depth - 1))
    return h.reshape(batch, seq, d), {name: jnp.stack(rows) for name, rows in new.items()}


def kernel(x_prompt, x_sample, p_prompt, p_sample, cache_sb_k, cache_sb_v, cache_mla_ckv, cache_mla_kpe,
           state_mlstm_c, state_mlstm_n, state_mlstm_m, page_table, norm_mix, norm_ffn, norm_ple, norm_final,
           w_in_ab, b_gates_ab, g_mlstm_head, w_out_ab, w_in_mla, g_q_lora, g_kv_lora, w_q_up, w_kv_up,
           w_out_mla, w_ffn_gate, w_ffn_up, w_ffn_down, w_ple_gate, w_ple_proj):
    wts = {
        'norm_mix': norm_mix, 'norm_ffn': norm_ffn, 'norm_ple': norm_ple, 'norm_final': norm_final,
        'b_gates_ab': b_gates_ab, 'g_mlstm_head': g_mlstm_head, 'g_q_lora': g_q_lora, 'g_kv_lora': g_kv_lora,
        'ab': [_prep_ab(w_in_ab[e], w_out_ab[e]) for e in range(w_in_ab.shape[0])],
        'mla': [_prep_mla(w_in_mla[o], w_q_up[o], w_kv_up[o]) for o in range(w_in_mla.shape[0])],
        'w_out_mla': w_out_mla.astype(BF16),
        'w_ffn_gate': w_ffn_gate.astype(BF16), 'w_ffn_up': w_ffn_up.astype(BF16),
        'w_ffn_down': w_ffn_down.astype(BF16),
        'w_ple_gate': w_ple_gate.astype(BF16), 'w_ple_proj': w_ple_proj.astype(BF16),
    }
    pos_prompt = jnp.arange(x_prompt.shape[1], dtype=jnp.int32)
    y_prompt, sp = _trunk(x_prompt, p_prompt, pos_prompt, wts, None)
    past_len = page_table.shape[1] * cache_sb_k.shape[2]
    past = {'sb_k': cache_sb_k, 'sb_v': cache_sb_v, 'ckv': cache_mla_ckv, 'kpe': cache_mla_kpe,
            'c': state_mlstm_c, 'n': state_mlstm_n, 'm': state_mlstm_m, 'page_table': page_table}
    pos_sample = past_len + jnp.arange(x_sample.shape[1], dtype=jnp.int32)
    y_sample, ss = _trunk(x_sample, p_sample, pos_sample, wts, past)
    return (y_prompt, y_sample,
            sp['sb_k'], sp['sb_v'], sp['c'], sp['n'], sp['m'], sp['ckv'], sp['kpe'],
            ss['sb_k'], ss['sb_v'], ss['c'], ss['n'], ss['m'], ss['ckv'], ss['kpe'])
```

```python
import functools

import jax
import jax.numpy as jnp
from jax import lax
from jax.experimental import pallas as pl
from jax.experimental.pallas import tpu as pltpu

F32 = jnp.float32
BF16 = jnp.bfloat16

EPS = 1e-6
H_A, DQK_A, DV_A, MLSTM_CHUNK = 4, 128, 256, 64
H_B, D_B = 8, 128
H_C, Q_LORA, KV_LORA, NOPE_DIM, ROPE_DIM, V_C = 16, 512, 512, 128, 64, 128
ROPE_THETA = 10000.0
SB_SCALE = D_B ** -0.5
MLA_SCALE = (NOPE_DIM + ROPE_DIM) ** -0.5
LANES = 128
SB_DONE = -104.0

AB_QA, AB_KA, AB_VA, AB_OA = 0, 512, 1024, 2048
AB_QB, AB_KB, AB_VB, AB_GATES = 3072, 4096, 5120, 6144
AB_WIDTH = AB_GATES + H_A * LANES
C_CQ, C_CKV, C_KPE, C_KROT = 0, 512, 1024, 1152
C_WIDTH = 1280
QU_NOPE, QU_PE, QU_ROT = 0, H_C * NOPE_DIM, H_C * NOPE_DIM + H_C * ROPE_DIM
QU_WIDTH = QU_ROT + H_C * ROPE_DIM
Q_WIDTH = QU_ROT

VMEM_LIMIT = 56 * 1024 * 1024


def _cparams(*sem):
    return pltpu.CompilerParams(dimension_semantics=sem, vmem_limit_bytes=VMEM_LIMIT)


def _dot(a, b):
    return jnp.dot(a, b, preferred_element_type=F32)


def _dot_nt(a, b):
    return lax.dot_general(a, b, (((1,), (1,)), ((), ())), preferred_element_type=F32)


def _dot_tn(a, b):
    return lax.dot_general(a, b, (((0,), (0,)), ((), ())), preferred_element_type=F32)


def _rms(x, g):
    return x * lax.rsqrt(jnp.mean(x * x, axis=-1, keepdims=True) + EPS) * g


def _softplus_neg_abs(z):
    return jnp.log(1.0 + jnp.exp(-jnp.abs(z)))


def _log_sigmoid(z):
    return jnp.minimum(z, 0.0) - _softplus_neg_abs(z)


def _sigmoid(z):
    return 1.0 / (1.0 + jnp.exp(-z))


def _tile(n, pref, align=8):
    if n <= pref:
        return n
    for t in range(pref - pref % align, 0, -align):
        if n % t == 0:
            return t
    raise ValueError((n, pref, align))


def _norm_matmul_body(x_ref, g_ref, w_ref, o_ref, xn_ref):
    @pl.when(pl.program_id(1) == 0)
    def _():
        xn_ref[...] = _rms(x_ref[...], g_ref[...]).astype(BF16)

    o_ref[...] = _dot(xn_ref[...], w_ref[...]).astype(o_ref.dtype)


def _norm_matmul(x, g, w, out_dtype, tm=512, tn=640):
    m, k = x.shape
    n = w.shape[1]
    tm, tn = _tile(m, tm), _tile(n, tn, LANES)
    return pl.pallas_call(
        _norm_matmul_body,
        grid=(m // tm, n // tn),
        in_specs=[pl.BlockSpec((tm, k), lambda i, j: (i, 0)),
                  pl.BlockSpec((1, k), lambda i, j: (0, 0)),
                  pl.BlockSpec((k, tn), lambda i, j: (0, j))],
        out_specs=pl.BlockSpec((tm, tn), lambda i, j: (i, j)),
        out_shape=jax.ShapeDtypeStruct((m, n), out_dtype),
        scratch_shapes=[pltpu.VMEM((tm, k), BF16)],
        compiler_params=_cparams("parallel", "arbitrary"),
        name="norm_matmul",
    )(x, g.reshape(1, k), w)


def _resid_matmul_body(*refs, n_in):
    h_ref, o_ref = refs[0], refs[-1]
    acc = h_ref[...]
    for i in range(n_in):
        acc = acc + _dot(refs[1 + 2 * i][...].astype(BF16), refs[2 + 2 * i][...])
    o_ref[...] = acc


def _resid_matmul(h, xs, ws, tm=512, tn=512):
    m, n = h.shape
    tm, tn = _tile(m, tm), _tile(n, tn, LANES)
    in_specs = [pl.BlockSpec((tm, tn), lambda i, j: (i, j))]
    args = [h]
    for x, w in zip(xs, ws):
        k = x.shape[1]
        in_specs += [pl.BlockSpec((tm, k), lambda i, j: (i, 0)),
                     pl.BlockSpec((k, tn), lambda i, j: (0, j))]
        args += [x, w]
    return pl.pallas_call(
        functools.partial(_resid_matmul_body, n_in=len(xs)),
        grid=(m // tm, n // tn),
        in_specs=in_specs,
        out_specs=pl.BlockSpec((tm, tn), lambda i, j: (i, j)),
        out_shape=jax.ShapeDtypeStruct((m, n), F32),
        compiler_params=_cparams("parallel", "arbitrary"),
        name="resid_matmul",
    )(*args)


def _ffn_body(h_ref, g_ref, wg_ref, wu_ref, wd_ref, o_ref, xn_ref):
    @pl.when(pl.program_id(1) == 0)
    def _():
        h = h_ref[...]
        xn_ref[...] = _rms(h, g_ref[...]).astype(BF16)
        o_ref[...] = h

    xn = xn_ref[...]
    a = _dot(xn, wg_ref[...])
    u = _dot(xn, wu_ref[...])
    act = (a * _sigmoid(a) * u).astype(BF16)
    o_ref[...] += _dot(act, wd_ref[...])


def _ffn(h, g, wg, wu, wd, tm=512, tf=512):
    m, d = h.shape
    f = wg.shape[1]
    tm, tf = _tile(m, tm), _tile(f, tf, LANES)
    return pl.pallas_call(
        _ffn_body,
        grid=(m // tm, f // tf),
        in_specs=[pl.BlockSpec((tm, d), lambda i, j: (i, 0)),
                  pl.BlockSpec((1, d), lambda i, j: (0, 0)),
                  pl.BlockSpec((d, tf), lambda i, j: (0, j)),
                  pl.BlockSpec((d, tf), lambda i, j: (0, j)),
                  pl.BlockSpec((tf, d), lambda i, j: (j, 0))],
        out_specs=pl.BlockSpec((tm, d), lambda i, j: (i, 0)),
        out_shape=jax.ShapeDtypeStruct((m, d), F32),
        scratch_shapes=[pltpu.VMEM((tm, d), BF16)],
        compiler_params=_cparams("parallel", "arbitrary"),
        name="ffn",
    )(h, g.reshape(1, d), wg, wu, wd)


def _ple_body(h_ref, p_ref, g_ref, wg_ref, wp_ref, gf_ref, o_ref, *, final):
    h = h_ref[...]
    xn = _rms(h, g_ref[...]).astype(BF16)
    gate = _sigmoid(_dot(xn, wg_ref[...]))
    out = h + gate * _dot(p_ref[...].astype(BF16), wp_ref[...])
    if final:
        out = _rms(out, gf_ref[...])
    o_ref[...] = out


def _ple(h, p, g, wg, wp, g_final, final, tm=256):
    m, d = h.shape
    pd = p.shape[1]
    tm = _tile(m, tm)
    return pl.pallas_call(
        functools.partial(_ple_body, final=final),
        grid=(m // tm,),
        in_specs=[pl.BlockSpec((tm, d), lambda i: (i, 0)),
                  pl.BlockSpec((tm, pd), lambda i: (i, 0)),
                  pl.BlockSpec((1, d), lambda i: (0, 0)),
                  pl.BlockSpec((d, d), lambda i: (0, 0)),
                  pl.BlockSpec((pd, d), lambda i: (0, 0)),
                  pl.BlockSpec((1, d), lambda i: (0, 0))],
        out_specs=pl.BlockSpec((tm, d), lambda i: (i, 0)),
        out_shape=jax.ShapeDtypeStruct((m, d), F32),
        compiler_params=_cparams("parallel"),
        name="ple",
    )(h, p, g.reshape(1, d), wg, wp, g_final.reshape(1, d))


def _mlstm_prompt_body(bias_ref, q_ref, k_ref, v_ref, oa_ref, gc_ref, gr_ref, gh_ref,
                       ha_ref, c_ref, n_ref, m_ref, cs, ns, ms, *, tb, chunk):
    h = pl.program_id(1)
    t = pl.program_id(2)

    @pl.when(t == 0)
    def _():
        cs[...] = jnp.zeros_like(cs)
        ns[...] = jnp.zeros_like(ns)
        ms[...] = jnp.zeros_like(ms)

    b_ig = bias_ref[h]
    b_fg = bias_ref[H_A + h]
    lane = lax.broadcasted_iota(jnp.int32, (tb, LANES), 1)
    gcb = gc_ref[...] + jnp.where(lane == 0, b_ig, b_fg)
    lsg_c = _log_sigmoid(gcb)
    row = lax.broadcasted_iota(jnp.int32, (8, tb), 0)
    grb = gr_ref[...] + jnp.where(row == 0, b_ig, b_fg)
    lsg_r = _log_sigmoid(grb)
    r = lax.broadcasted_iota(jnp.int32, (tb, tb), 0)
    c = lax.broadcasted_iota(jnp.int32, (tb, tb), 1)
    shift = chunk.bit_length() - 1
    same = jnp.right_shift(r, shift) == jnp.right_shift(c, shift)
    tril = jnp.where(same, jnp.where(r >= c, 1.0, 0.0), 0.0).astype(F32)
    bc_all = jnp.dot(tril, lsg_c, precision=lax.Precision.HIGHEST, preferred_element_type=F32)
    br_all = lax.dot_general(lsg_r, tril, (((1,), (1,)), ((), ())),
                             precision=lax.Precision.HIGHEST, preferred_element_type=F32)
    tt = lax.broadcasted_iota(jnp.int32, (chunk, chunk), 0)
    ss = lax.broadcasted_iota(jnp.int32, (chunk, chunk), 1)
    causal = ss <= tt
    gh = gh_ref[...]

    for ci in range(tb // chunk):
        lo, hi = ci * chunk, (ci + 1) * chunk
        bcol = bc_all[lo:hi, 1:2]
        igcol = gcb[lo:hi, 0:1]
        brow = br_all[1:2, lo:hi]
        igrow = grb[0:1, lo:hi]
        m_prev = ms[...]
        d = jnp.where(causal, bcol - brow + igrow, -jnp.inf)
        g = bcol + m_prev
        m_t = jnp.maximum(g, jnp.max(d, axis=-1, keepdims=True))
        w_in = jnp.exp(g - m_t)
        qf = q_ref[lo:hi, :]
        qb = qf.astype(BF16)
        kf = k_ref[lo:hi, :] * (DQK_A ** -0.5)
        kb = kf.astype(BF16)
        vb = v_ref[lo:hi, :].astype(BF16)
        s = _dot_nt(qb, kb) * jnp.exp(d - m_t)
        cmat = cs[...]
        nvec = ns[...]
        num = w_in * _dot(qb, cmat.astype(BF16)) + _dot(s.astype(BF16), vb)
        den = w_in * jnp.sum(qf * nvec, axis=-1, keepdims=True) + jnp.sum(s, axis=-1, keepdims=True)
        hh = num / jnp.maximum(jnp.abs(den), jnp.exp(-m_t))
        m_new = m_t[chunk - 1:chunk, :]
        decay = jnp.exp(g[chunk - 1:chunk, :] - m_new)
        wk = jnp.exp(bcol[chunk - 1:chunk, :] - bcol + igcol - m_new)
        kw = kf * wk
        cs[...] = decay * cmat + _dot_tn(kw.astype(BF16), vb)
        ns[...] = decay * nvec + jnp.sum(kw, axis=0, keepdims=True)
        ms[...] = m_new
        hn = _rms(hh, gh)
        ha_ref[lo:hi, :] = (hn * _sigmoid(oa_ref[lo:hi, :])).astype(ha_ref.dtype)

    @pl.when(t == pl.num_programs(2) - 1)
    def _():
        c_ref[...] = cs[...]
        n_ref[...] = ns[...]
        m_ref[...] = ms[...]


def _mlstm_prompt(proj, gates_t, b_gates, g_head, batch, seq):
    chunk = MLSTM_CHUNK if seq % MLSTM_CHUNK == 0 else seq
    tb = _tile(seq, 512) if seq % MLSTM_CHUNK == 0 else seq
    nt = seq // tb
    m = batch * seq
    qo, ko, vo, oo, go = (AB_QA // DQK_A, AB_KA // DQK_A, AB_VA // DV_A, AB_OA // DV_A, AB_GATES // LANES)
    return pl.pallas_call(
        functools.partial(_mlstm_prompt_body, tb=tb, chunk=chunk),
        grid=(batch, H_A, nt),
        in_specs=[pl.BlockSpec(memory_space=pltpu.SMEM),
                  pl.BlockSpec((tb, DQK_A), lambda b, h, t: (b * nt + t, qo + h)),
                  pl.BlockSpec((tb, DQK_A), lambda b, h, t: (b * nt + t, ko + h)),
                  pl.BlockSpec((tb, DV_A), lambda b, h, t: (b * nt + t, vo + h)),
                  pl.BlockSpec((tb, DV_A), lambda b, h, t: (b * nt + t, oo + h)),
                  pl.BlockSpec((tb, LANES), lambda b, h, t: (b * nt + t, go + h)),
                  pl.BlockSpec((None, 8, tb), lambda b, h, t: (h, 0, b * nt + t)),
                  pl.BlockSpec((1, DV_A), lambda b, h, t: (0, h))],
        out_specs=[pl.BlockSpec((tb, DV_A), lambda b, h, t: (b * nt + t, h)),
                   pl.BlockSpec((None, None, DQK_A, DV_A), lambda b, h, t: (b, h, 0, 0)),
                   pl.BlockSpec((None, None, 1, DQK_A), lambda b, h, t: (b, h, 0, 0)),
                   pl.BlockSpec((None, None, 1, 1), lambda b, h, t: (b, h, 0, 0))],
        out_shape=[jax.ShapeDtypeStruct((m, H_A * DV_A), BF16),
                   jax.ShapeDtypeStruct((batch, H_A, DQK_A, DV_A), F32),
                   jax.ShapeDtypeStruct((batch, H_A, 1, DQK_A), F32),
                   jax.ShapeDtypeStruct((batch, H_A, 1, 1), F32)],
        scratch_shapes=[pltpu.VMEM((DQK_A, DV_A), F32), pltpu.VMEM((1, DQK_A), F32), pltpu.VMEM((1, 1), F32)],
        compiler_params=_cparams("parallel", "parallel", "arbitrary"),
        name="mlstm_prompt",
    )(b_gates, proj, proj, proj, proj, proj, gates_t, g_head.reshape(1, H_A * DV_A))


def _mlstm_decode_body(bias_ref, q_ref, k_ref, v_ref, oa_ref, gc_ref, gh_ref, c0_ref, n0_ref, m0_ref,
                       ha_ref, c_ref, n_ref, m_ref, *, tb):
    h = pl.program_id(1)
    b_ig = bias_ref[h]
    b_fg = bias_ref[H_A + h]
    gc = gc_ref[...]
    ig = gc[:, 0:1] + b_ig
    fg = _log_sigmoid(gc[:, 1:2] + b_fg)
    m0 = m0_ref[...]
    g = fg + m0
    m_t = jnp.maximum(g, ig)
    w_in = jnp.exp(g - m_t)
    e_ig = jnp.exp(ig - m_t)
    qf = q_ref[...]
    kf = k_ref[...] * (DQK_A ** -0.5)
    vf = v_ref[...]
    vb = vf.astype(BF16)
    qk = jnp.sum(qf.astype(BF16).astype(F32) * kf.astype(BF16).astype(F32), axis=-1, keepdims=True)
    s = qk * e_ig
    kw = kf * e_ig
    rows = lax.broadcasted_iota(jnp.int32, (tb, 1), 0)

    def step(b, hq):
        sel = rows == b
        cb = c0_ref[b]
        hq = hq + _dot(jnp.where(sel, qf, 0.0).astype(BF16), cb.astype(BF16))
        upd = _dot_tn(jnp.where(sel, kw, 0.0).astype(BF16), vb)
        decay = jnp.sum(jnp.where(sel, w_in, 0.0), axis=0, keepdims=True)
        c_ref[b] = decay * cb + upd
        return hq

    hq = lax.fori_loop(0, tb, step, jnp.zeros((tb, DV_A), F32))
    n0 = n0_ref[...]
    num = w_in * hq + s * vb.astype(F32)
    den = w_in * jnp.sum(qf * n0, axis=-1, keepdims=True) + s
    hh = num / jnp.maximum(jnp.abs(den), jnp.exp(-m_t))
    n_ref[...] = w_in * n0 + kw
    m_ref[...] = m_t
    hn = _rms(hh, gh_ref[...])
    ha_ref[...] = (hn * _sigmoid(oa_ref[...])).astype(ha_ref.dtype)


def _mlstm_decode(proj, b_gates, g_head, c0, n0, m0, layer):
    batch = proj.shape[0]
    tb = _tile(batch, 32)
    qo, ko, vo, oo, go = (AB_QA // DQK_A, AB_KA // DQK_A, AB_VA // DV_A, AB_OA // DV_A, AB_GATES // LANES)
    return pl.pallas_call(
        functools.partial(_mlstm_decode_body, tb=tb),
        grid=(batch // tb, H_A),
        in_specs=[pl.BlockSpec(memory_space=pltpu.SMEM),
                  pl.BlockSpec((tb, DQK_A), lambda i, h: (i, qo + h)),
                  pl.BlockSpec((tb, DQK_A), lambda i, h: (i, ko + h)),
                  pl.BlockSpec((tb, DV_A), lambda i, h: (i, vo + h)),
                  pl.BlockSpec((tb, DV_A), lambda i, h: (i, oo + h)),
                  pl.BlockSpec((tb, LANES), lambda i, h: (i, go + h)),
                  pl.BlockSpec((1, DV_A), lambda i, h: (0, h)),
                  pl.BlockSpec((None, tb, None, DQK_A, DV_A), lambda i, h: (layer, i, h, 0, 0)),
                  pl.BlockSpec((tb, DQK_A), lambda i, h: (i, h)),
                  pl.BlockSpec((None, tb, 1), lambda i, h: (h, i, 0))],
        out_specs=[pl.BlockSpec((tb, DV_A), lambda i, h: (i, h)),
                   pl.BlockSpec((tb, None, DQK_A, DV_A), lambda i, h: (i, h, 0, 0)),
                   pl.BlockSpec((tb, DQK_A), lambda i, h: (i, h)),
                   pl.BlockSpec((None, tb, 1), lambda i, h: (h, i, 0))],
        out_shape=[jax.ShapeDtypeStruct((batch, H_A * DV_A), BF16),
                   jax.ShapeDtypeStruct((batch, H_A, DQK_A, DV_A), F32),
                   jax.ShapeDtypeStruct((batch, H_A * DQK_A), F32),
                   jax.ShapeDtypeStruct((H_A, batch, 1), F32)],
        compiler_params=_cparams("parallel", "parallel"),
        name="mlstm_decode",
    )(b_gates, proj, proj, proj, proj, proj, g_head.reshape(1, H_A * DV_A), c0, n0, m0)


def _sb_weights(z, mask, tri, carry):
    sp = _softplus_neg_abs(z)
    ls = jnp.minimum(z, 0.0) - sp
    lk = -(jnp.maximum(z, 0.0) + sp)
    if mask is not None:
        lk = jnp.where(mask, lk, 0.0)
    hi = lk.astype(BF16)
    lo = (lk - hi.astype(F32)).astype(BF16)
    la = _dot(hi, tri) + _dot(lo, tri) + carry
    a = jnp.exp(ls + la)
    if mask is not None:
        a = jnp.where(mask, a, 0.0)
    return a, carry + jnp.sum(lk, axis=-1, keepdims=True)


def _later_key_matrix(n):
    r = lax.broadcasted_iota(jnp.int32, (n, n), 0)
    c = lax.broadcasted_iota(jnp.int32, (n, n), 1)
    return jnp.where(r > c, 1.0, 0.0).astype(BF16)


def _sb_prompt_body(q_ref, k_ref, v_ref, o_ref, acc_ref, carry_ref, done_ref, *, tq):
    i = pl.program_id(1)
    jj = pl.program_id(2)
    j = i - jj

    @pl.when(jj == 0)
    def _():
        acc_ref[...] = jnp.zeros_like(acc_ref)
        carry_ref[...] = jnp.zeros_like(carry_ref)
        done_ref[0] = 0

    @pl.when(jnp.logical_and(j >= 0, done_ref[0] == 0))
    def _():
        qpos = i * tq + lax.broadcasted_iota(jnp.int32, (tq, tq), 0)
        kpos = j * tq + lax.broadcasted_iota(jnp.int32, (tq, tq), 1)
        mask = kpos < qpos
        tri = _later_key_matrix(tq)
        for h in range(H_B):
            sl = slice(h * D_B, (h + 1) * D_B)
            z = _dot_nt(q_ref[:, sl].astype(BF16), k_ref[:, sl].astype(BF16)) * SB_SCALE
            a, carry = _sb_weights(z, mask, tri, carry_ref[h])
            acc_ref[h] += _dot(a.astype(BF16), v_ref[:, sl].astype(BF16))
            carry_ref[h] = carry
        done_ref[0] = (jnp.max(carry_ref[...]) < SB_DONE).astype(jnp.int32)

    @pl.when(jj == pl.num_programs(2) - 1)
    def _():
        for h in range(H_B):
            o_ref[:, h * D_B:(h + 1) * D_B] = acc_ref[h].astype(o_ref.dtype)


def _sb_prompt(proj, batch, seq):
    tq = _tile(seq, 256)
    nq = seq // tq
    width = H_B * D_B
    qo, ko, vo = AB_QB // width, AB_KB // width, AB_VB // width
    return pl.pallas_call(
        functools.partial(_sb_prompt_body, tq=tq),
        grid=(batch, nq, nq),
        in_specs=[pl.BlockSpec((tq, width), lambda b, i, jj: (b * nq + i, qo)),
                  pl.BlockSpec((tq, width), lambda b, i, jj: (b * nq + jnp.maximum(i - jj, 0), ko)),
                  pl.BlockSpec((tq, width), lambda b, i, jj: (b * nq + jnp.maximum(i - jj, 0), vo))],
        out_specs=pl.BlockSpec((tq, width), lambda b, i, jj: (b * nq + i, 0)),
        out_shape=jax.ShapeDtypeStruct((batch * seq, width), BF16),
        scratch_shapes=[pltpu.VMEM((H_B, tq, D_B), F32), pltpu.VMEM((H_B, tq, 1), F32),
                        pltpu.SMEM((1,), jnp.int32)],
        compiler_params=_cparams("parallel", "parallel", "arbitrary"),
        name="sb_prompt",
    )(proj, proj, proj)


def _sb_decode_body(pt_ref, q_ref, k_hbm, v_hbm, o_ref, fk, fv, kb, vb, fsem, bsem, acc_ref, carry_ref,
                    *, n_pages, page, layer):
    b = pl.program_id(0)
    nb = pl.num_programs(0)

    def first_copies(seq, slot):
        pg = pt_ref[seq, n_pages - 1]
        return (pltpu.make_async_copy(k_hbm.at[layer, pg], fk.at[slot], fsem.at[0, slot]),
                pltpu.make_async_copy(v_hbm.at[layer, pg], fv.at[slot], fsem.at[1, slot]))

    def page_copies(p, slot):
        pg = pt_ref[b, p]
        return (pltpu.make_async_copy(k_hbm.at[layer, pg], kb.at[slot], bsem.at[0, slot]),
                pltpu.make_async_copy(v_hbm.at[layer, pg], vb.at[slot], bsem.at[1, slot]))

    @pl.when(b == 0)
    def _():
        for c in first_copies(0, 0):
            c.start()

    @pl.when(b + 1 < nb)
    def _():
        for c in first_copies(b + 1, (b + 1) % 2):
            c.start()

    if n_pages > 1:
        for c in page_copies(n_pages - 2, 0):
            c.start()

    acc_ref[...] = jnp.zeros_like(acc_ref)
    carry_ref[...] = jnp.zeros_like(carry_ref)
    q = q_ref[...]
    qall = jnp.concatenate([q[:, h * D_B:(h + 1) * D_B] for h in range(H_B)], axis=0).astype(BF16)
    head = lax.broadcasted_iota(jnp.int32, (H_B, page), 0)
    head_d = lax.broadcasted_iota(jnp.int32, (H_B, D_B), 0)
    tri = _later_key_matrix(page)

    def sweep_page(k_ref, v_ref):
        z = jnp.zeros((H_B, page), F32)
        for h in range(H_B):
            z = jnp.where(head == h, _dot_nt(qall, k_ref[:, h, :].astype(BF16)), z)
        a, carry = _sb_weights(z * SB_SCALE, None, tri, carry_ref[...])
        ab = a.astype(BF16)
        upd = jnp.zeros((H_B, D_B), F32)
        for h in range(H_B):
            upd = jnp.where(head_d == h, _dot(ab, v_ref[:, h, :].astype(BF16)), upd)
        acc_ref[...] += upd
        carry_ref[...] = carry
        return jnp.max(carry) < SB_DONE

    fslot = b % 2
    for c in first_copies(b, fslot):
        c.wait()
    done0 = sweep_page(fk.at[fslot], fv.at[fslot])

    def more(st):
        p, _, done = st
        return jnp.logical_and(p >= 0, jnp.logical_not(done))

    def step(st):
        p, slot, _ = st
        for c in page_copies(p, slot):
            c.wait()

        @pl.when(p >= 1)
        def _():
            for c in page_copies(p - 1, 1 - slot):
                c.start()

        return p - 1, 1 - slot, sweep_page(kb.at[slot], vb.at[slot])

    p, slot, _ = lax.while_loop(more, step, (jnp.int32(n_pages - 2), jnp.int32(0), done0))

    @pl.when(p >= 0)
    def _():
        for c in page_copies(p, slot):
            c.wait()

    for h in range(H_B):
        o_ref[:, h * D_B:(h + 1) * D_B] = acc_ref[h:h + 1, :].astype(o_ref.dtype)


def _sb_decode(q, cache_k, cache_v, page_table, layer):
    batch = q.shape[0]
    n_pages = page_table.shape[1]
    page = cache_k.shape[2]
    page_buf = pltpu.VMEM((2, page, H_B, D_B), cache_k.dtype)
    grid_spec = pltpu.PrefetchScalarGridSpec(
        num_scalar_prefetch=1,
        grid=(batch,),
        in_specs=[pl.BlockSpec((None, 1, H_B * D_B), lambda b, pt: (b, 0, 0)),
                  pl.BlockSpec(memory_space=pl.ANY),
                  pl.BlockSpec(memory_space=pl.ANY)],
        out_specs=pl.BlockSpec((None, 1, H_B * D_B), lambda b, pt: (b, 0, 0)),
        scratch_shapes=[page_buf, page_buf, page_buf, page_buf,
                        pltpu.SemaphoreType.DMA((2, 2)), pltpu.SemaphoreType.DMA((2, 2)),
                        pltpu.VMEM((H_B, D_B), F32), pltpu.VMEM((H_B, 1), F32)],
    )
    return pl.pallas_call(
        functools.partial(_sb_decode_body, n_pages=n_pages, page=page, layer=layer),
        grid_spec=grid_spec,
        out_shape=jax.ShapeDtypeStruct((batch, 1, H_B * D_B), BF16),
        compiler_params=_cparams("arbitrary"),
        name="sb_decode",
    )(page_table, q, cache_k, cache_v)


def _mla_up_body(pr_ref, gq_ref, gkv_ref, wq_ref, wkv_ref, cos_ref, sin_ref, *out_refs, with_kv):
    q_ref, ckv_ref, kpe_ref = out_refs[:3]
    cos = cos_ref[...]
    sin = sin_ref[...]
    cq = _rms(pr_ref[:, C_CQ:C_CQ + Q_LORA], gq_ref[...]).astype(BF16)
    ckv = _rms(pr_ref[:, C_CKV:C_CKV + KV_LORA], gkv_ref[...])
    ckv_ref[...] = ckv
    kpe_ref[...] = pr_ref[:, C_KPE:C_KPE + LANES] * cos + pr_ref[:, C_KROT:C_KROT + LANES] * sin
    qa = _dot(cq, wq_ref[...])
    q_ref[:, QU_NOPE:QU_PE] = qa[:, QU_NOPE:QU_PE].astype(BF16)
    for c in range((QU_ROT - QU_PE) // LANES):
        pe = qa[:, QU_PE + c * LANES:QU_PE + (c + 1) * LANES]
        rot = qa[:, QU_ROT + c * LANES:QU_ROT + (c + 1) * LANES]
        q_ref[:, QU_PE + c * LANES:QU_PE + (c + 1) * LANES] = (pe * cos + rot * sin).astype(BF16)
    if with_kv:
        out_refs[3][...] = _dot(ckv.astype(BF16), wkv_ref[...]).astype(BF16)


def _mla_up(proj, g_q, g_kv, w_q, w_kv, cos, sin, with_kv, tm=256):
    m = proj.shape[0]
    tm = _tile(m, tm)
    kv_w = w_kv.shape[1]
    out_specs = [pl.BlockSpec((tm, Q_WIDTH), lambda i: (i, 0)),
                 pl.BlockSpec((tm, KV_LORA), lambda i: (i, 0)),
                 pl.BlockSpec((tm, LANES), lambda i: (i, 0))]
    out_shape = [jax.ShapeDtypeStruct((m, Q_WIDTH), BF16),
                 jax.ShapeDtypeStruct((m, KV_LORA), F32),
                 jax.ShapeDtypeStruct((m, LANES), F32)]
    if with_kv:
        out_specs.append(pl.BlockSpec((tm, kv_w), lambda i: (i, 0)))
        out_shape.append(jax.ShapeDtypeStruct((m, kv_w), BF16))
    return pl.pallas_call(
        functools.partial(_mla_up_body, with_kv=with_kv),
        grid=(m // tm,),
        in_specs=[pl.BlockSpec((tm, C_WIDTH), lambda i: (i, 0)),
                  pl.BlockSpec((1, Q_LORA), lambda i: (0, 0)),
                  pl.BlockSpec((1, KV_LORA), lambda i: (0, 0)),
                  pl.BlockSpec((Q_LORA, QU_WIDTH), lambda i: (0, 0)),
                  pl.BlockSpec((KV_LORA, kv_w), lambda i: (0, 0)),
                  pl.BlockSpec((tm, LANES), lambda i: (i, 0)),
                  pl.BlockSpec((tm, LANES), lambda i: (i, 0))],
        out_specs=out_specs,
        out_shape=out_shape,
        compiler_params=_cparams("parallel"),
        name="mla_up",
    )(proj, g_q.reshape(1, Q_LORA), g_kv.reshape(1, KV_LORA), w_q, w_kv, cos, sin)


def _mla_prompt_body(q_ref, kn_ref, v_ref, kpe_ref, o_ref, m_ref, l_ref, acc_ref, *, tq):
    i = pl.program_id(1)
    j = pl.program_id(2)

    @pl.when(j == 0)
    def _():
        m_ref[...] = jnp.full_like(m_ref, -1e30)
        l_ref[...] = jnp.zeros_like(l_ref)
        acc_ref[...] = jnp.zeros_like(acc_ref)

    @pl.when(j <= i)
    def _():
        qpos = i * tq + lax.broadcasted_iota(jnp.int32, (tq, tq), 0)
        kpos = j * tq + lax.broadcasted_iota(jnp.int32, (tq, tq), 1)
        mask = kpos <= qpos
        kpe = kpe_ref[...]
        lane = lax.broadcasted_iota(jnp.int32, kpe.shape, 1)
        kpe_sel = (jnp.where(lane < ROPE_DIM, kpe, 0.0).astype(BF16),
                   jnp.where(lane >= ROPE_DIM, kpe, 0.0).astype(BF16))
        for h in range(H_C):
            sl = slice(h * NOPE_DIM, (h + 1) * NOPE_DIM)
            pe = slice(QU_PE + (h // 2) * LANES, QU_PE + (h // 2 + 1) * LANES)
            qcat = jnp.concatenate([q_ref[:, sl], q_ref[:, pe]], axis=1)
            kcat = jnp.concatenate([kn_ref[:, sl], kpe_sel[h % 2]], axis=1)
            s = jnp.where(mask, _dot_nt(qcat, kcat) * MLA_SCALE, -jnp.inf)
            m_prev = m_ref[h]
            m_new = jnp.maximum(m_prev, jnp.max(s, axis=-1, keepdims=True))
            alpha = jnp.exp(m_prev - m_new)
            p = jnp.exp(s - m_new)
            l_ref[h] = alpha * l_ref[h] + jnp.sum(p, axis=-1, keepdims=True)
            acc_ref[h] = alpha * acc_ref[h] + _dot(p.astype(BF16), v_ref[:, sl])
            m_ref[h] = m_new

    @pl.when(j == pl.num_programs(2) - 1)
    def _():
        for h in range(H_C):
            o_ref[:, h * V_C:(h + 1) * V_C] = (acc_ref[h] / l_ref[h]).astype(o_ref.dtype)


def _mla_prompt(q, kv, kpe2, batch, seq):
    tq = _tile(seq, 512)
    nq = seq // tq
    kw, vw = H_C * NOPE_DIM, H_C * V_C
    assert kw == vw
    return pl.pallas_call(
        functools.partial(_mla_prompt_body, tq=tq),
        grid=(batch, nq, nq),
        in_specs=[pl.BlockSpec((tq, Q_WIDTH), lambda b, i, j: (b * nq + i, 0)),
                  pl.BlockSpec((tq, kw), lambda b, i, j: (b * nq + jnp.minimum(i, j), 0)),
                  pl.BlockSpec((tq, vw), lambda b, i, j: (b * nq + jnp.minimum(i, j), 1)),
                  pl.BlockSpec((tq, LANES), lambda b, i, j: (b * nq + jnp.minimum(i, j), 0))],
        out_specs=pl.BlockSpec((tq, vw), lambda b, i, j: (b * nq + i, 0)),
        out_shape=jax.ShapeDtypeStruct((batch * seq, vw), BF16),
        scratch_shapes=[pltpu.VMEM((H_C, tq, 1), F32), pltpu.VMEM((H_C, tq, 1), F32),
                        pltpu.VMEM((H_C, tq, V_C), F32)],
        compiler_params=_cparams("parallel", "parallel", "arbitrary"),
        name="mla_prompt",
    )(q, kv, kv, kpe2)


def _head_matmul_body(x_ref, w_ref, o_ref):
    o_ref[...] = _dot(x_ref[...], w_ref[...]).astype(o_ref.dtype)


def _head_matmul(x, w, d_in, col0=0):
    m = x.shape[0]
    n_h, _, d_out = w.shape
    off = col0 // d_in
    return pl.pallas_call(
        _head_matmul_body,
        grid=(n_h,),
        in_specs=[pl.BlockSpec((m, d_in), lambda h: (0, off + h)),
                  pl.BlockSpec((None, d_in, d_out), lambda h: (h, 0, 0))],
        out_specs=pl.BlockSpec((m, d_out), lambda h: (0, h)),
        out_shape=jax.ShapeDtypeStruct((m, n_h * d_out), BF16),
        compiler_params=_cparams("parallel"),
        name="head_matmul",
    )(x, w)


def _mla_decode_body(pt_ref, ql_ref, qp_ref, cn_ref, kn_ref, *refs, pages_per_step, page):
    del pt_ref
    g_n = pages_per_step
    c_refs, p_refs = refs[:g_n], refs[g_n:2 * g_n]
    o_ref, kbuf, pbuf, m_ref, l_ref, acc_ref = refs[2 * g_n:]
    jj = pl.program_id(1)
    ql = ql_ref[...]
    qp = qp_ref[...]

    @pl.when(jj == 0)
    def _():
        lat = cn_ref[...].astype(BF16).astype(F32)
        kpe = kn_ref[...].astype(BF16).astype(F32)
        s0 = (jnp.sum(ql.astype(F32) * lat, axis=-1, keepdims=True)
              + jnp.sum(qp.astype(F32) * kpe, axis=-1, keepdims=True)) * MLA_SCALE
        m_ref[...] = s0
        l_ref[...] = jnp.ones_like(l_ref)
        acc_ref[...] = jnp.broadcast_to(lat, acc_ref.shape)

    for g in range(g_n):
        kbuf[g * page:(g + 1) * page, :] = c_refs[g][...].astype(BF16)
        pbuf[:, g * page:(g + 1) * page] = p_refs[g][...].astype(BF16)
    s = (_dot_nt(ql, kbuf[...]) + _dot(qp, pbuf[...])) * MLA_SCALE
    m_prev = m_ref[...]
    m_new = jnp.maximum(m_prev, jnp.max(s, axis=-1, keepdims=True))
    alpha = jnp.exp(m_prev - m_new)
    p = jnp.exp(s - m_new)
    l_ref[...] = alpha * l_ref[...] + jnp.sum(p, axis=-1, keepdims=True)
    acc_ref[...] = alpha * acc_ref[...] + _dot(p.astype(BF16), kbuf[...])
    m_ref[...] = m_new

    @pl.when(jj == pl.num_programs(1) - 1)
    def _():
        o_ref[...] = (acc_ref[...] / l_ref[...]).astype(o_ref.dtype)


def _mla_decode(q_lat, q_pe, ckv_new, kpe_new, cache_ckv, cache_kpe_t, page_table, layer):
    batch = q_lat.shape[0]
    n_pages = page_table.shape[1]
    page = cache_ckv.shape[2]
    g_n = _tile(n_pages, 32, 1)
    steps = n_pages // g_n

    def page_spec(rows, cols, g):
        return pl.BlockSpec((None, None, rows, cols), lambda b, jj, pt: (layer, pt[b, jj * g_n + g], 0, 0))

    grid_spec = pltpu.PrefetchScalarGridSpec(
        num_scalar_prefetch=1,
        grid=(batch, steps),
        in_specs=[pl.BlockSpec((None, H_C, KV_LORA), lambda b, jj, pt: (b, 0, 0)),
                  pl.BlockSpec((None, H_C, ROPE_DIM), lambda b, jj, pt: (b, 0, 0)),
                  pl.BlockSpec((None, 1, KV_LORA), lambda b, jj, pt: (b, 0, 0)),
                  pl.BlockSpec((None, 1, ROPE_DIM), lambda b, jj, pt: (b, 0, 0))]
        + [page_spec(page, KV_LORA, g) for g in range(g_n)]
        + [page_spec(ROPE_DIM, page, g) for g in range(g_n)],
        out_specs=pl.BlockSpec((None, H_C, KV_LORA), lambda b, jj, pt: (b, 0, 0)),
        scratch_shapes=[pltpu.VMEM((g_n * page, KV_LORA), BF16),
                        pltpu.VMEM((ROPE_DIM, g_n * page), BF16),
                        pltpu.VMEM((H_C, 1), F32), pltpu.VMEM((H_C, 1), F32),
                        pltpu.VMEM((H_C, KV_LORA), F32)],
    )
    return pl.pallas_call(
        functools.partial(_mla_decode_body, pages_per_step=g_n, page=page),
        grid_spec=grid_spec,
        out_shape=jax.ShapeDtypeStruct((batch, H_C, KV_LORA), BF16),
        compiler_params=_cparams("parallel", "arbitrary"),
        name="mla_decode",
    )(page_table, q_lat, q_pe, ckv_new, kpe_new, *([cache_ckv] * g_n), *([cache_kpe_t] * g_n))


def _rot_cols(w):
    k = w.shape[0]
    w4 = w.reshape(k, -1, 2, ROPE_DIM // 2)
    return jnp.stack([-w4[:, :, 1], w4[:, :, 0]], axis=2).reshape(k, -1)


def _prep_ab(w_in, w_out):
    d = w_in.shape[0]
    sizes = (H_A * DQK_A, H_A * DQK_A, H_A * DV_A, H_A * DV_A, 2 * H_A, H_B * D_B, H_B * D_B, H_B * D_B)
    parts, start = [], 0
    for sz in sizes:
        parts.append(w_in[:, start:start + sz])
        start += sz
    qa, ka, va, oa, gates, qb, kb, vb = parts
    gcols = jnp.zeros((d, H_A, LANES), w_in.dtype)
    gcols = gcols.at[:, :, 0].set(gates[:, :H_A]).at[:, :, 1].set(gates[:, H_A:])
    w_ext = jnp.concatenate([qa, ka, va, oa, qb, kb, vb, gcols.reshape(d, H_A * LANES)], axis=1).astype(BF16)
    w_out = w_out.astype(BF16)
    return w_ext, w_out[:H_A * DV_A], w_out[H_A * DV_A:]


def _prep_mla(w_in, w_q_up, w_kv_up):
    kpe_w = w_in[:, Q_LORA + KV_LORA:]
    rot_w = _rot_cols(kpe_w)
    w_in_ext = jnp.concatenate([w_in[:, :Q_LORA + KV_LORA], kpe_w, kpe_w, rot_w, rot_w], axis=1).astype(BF16)
    wq = w_q_up.reshape(Q_LORA, H_C, NOPE_DIM + ROPE_DIM)
    wq_pe = wq[:, :, NOPE_DIM:].reshape(Q_LORA, H_C * ROPE_DIM)
    w_q_ext = jnp.concatenate([wq[:, :, :NOPE_DIM].reshape(Q_LORA, H_C * NOPE_DIM), wq_pe, _rot_cols(wq_pe)],
                              axis=1).astype(BF16)
    wkv = w_kv_up.reshape(KV_LORA, H_C, NOPE_DIM + V_C)
    w_uk, w_uv = wkv[:, :, :NOPE_DIM], wkv[:, :, NOPE_DIM:]
    w_kv_ext = jnp.concatenate([w_uk.reshape(KV_LORA, -1), w_uv.reshape(KV_LORA, -1)], axis=1).astype(BF16)
    w_uk_t = jnp.transpose(w_uk, (1, 2, 0)).astype(BF16)
    w_uv_h = jnp.transpose(w_uv, (1, 0, 2)).astype(BF16)
    return w_in_ext, w_q_ext, w_kv_ext, w_uk_t, w_uv_h


def _rope_tables(pos):
    half = ROPE_DIM // 2
    freqs = ROPE_THETA ** (-jnp.arange(half, dtype=F32) / half)
    ang = pos.astype(F32)[:, None] * freqs[None, :]
    reps = LANES // half
    return jnp.tile(jnp.cos(ang), (1, reps)), jnp.tile(jnp.sin(ang), (1, reps))


def _trunk(x, ple, pos, wts, past):
    batch, seq, d = x.shape
    m = batch * seq
    h = x.reshape(m, d)
    depth = wts['norm_mix'].shape[0]
    cos, sin = _rope_tables(pos)
    cos, sin = jnp.tile(cos, (batch, 1)), jnp.tile(sin, (batch, 1))
    new = {name: [] for name in ('sb_k', 'sb_v', 'c', 'n', 'm', 'ckv', 'kpe')}
    for i in range(depth):
        if i % 2 == 0:
            e = i // 2
            w_ext, w_out_a, w_out_b = wts['ab'][e]
            proj = _norm_matmul(h, wts['norm_mix'][i], w_ext, F32)
            kb = proj[:, AB_KB:AB_KB + H_B * D_B]
            vb = proj[:, AB_VB:AB_VB + H_B * D_B]
            b_gates = wts['b_gates_ab'][e].astype(F32)
            g_head = wts['g_mlstm_head'][e].astype(F32)
            if past is None:
                gates_t = proj[:, AB_GATES:].reshape(m, H_A, LANES)[:, :, :8]
                gates_t = jnp.transpose(gates_t, (1, 2, 0))
                ha, c_new, n_new, m_new = _mlstm_prompt(proj, gates_t, b_gates, g_head, batch, seq)
                n_new = n_new.reshape(batch, H_A, DQK_A)
                m_new = m_new.reshape(batch, H_A)
                hb = _sb_prompt(proj, batch, seq)
            else:
                n0 = past['n'][e].astype(F32).reshape(batch, H_A * DQK_A)
                m0 = jnp.transpose(past['m'][e].astype(F32))[:, :, None]
                ha, c_new, n_new, m_new = _mlstm_decode(proj, b_gates, g_head, past['c'], n0, m0, e)
                n_new = n_new.reshape(batch, H_A, DQK_A)
                m_new = jnp.transpose(m_new[:, :, 0])
                qb = proj[:, AB_QB:AB_QB + H_B * D_B].reshape(batch, 1, H_B * D_B)
                hb = _sb_decode(qb, past['sb_k'], past['sb_v'], past['page_table'], e)
                hb = hb.reshape(batch, H_B * D_B)
            h = _resid_matmul(h, [ha, hb], [w_out_a, w_out_b])
            new['sb_k'].append(kb.reshape(batch, seq, H_B, D_B))
            new['sb_v'].append(vb.reshape(batch, seq, H_B, D_B))
            new['c'].append(c_new)
            new['n'].append(n_new)
            new['m'].append(m_new)
        else:
            o = i // 2
            w_in_ext, w_q_ext, w_kv_ext, w_uk_t, w_uv_h = wts['mla'][o]
            proj = _norm_matmul(h, wts['norm_mix'][i], w_in_ext, F32)
            g_q, g_kv = wts['g_q_lora'][o], wts['g_kv_lora'][o]
            if past is None:
                q, ckv, kpe2, kv = _mla_up(proj, g_q, g_kv, w_q_ext, w_kv_ext, cos, sin, True)
                att = _mla_prompt(q, kv, kpe2, batch, seq)
            else:
                q, ckv, kpe2 = _mla_up(proj, g_q, g_kv, w_q_ext, w_kv_ext, cos, sin, False)
                q_lat = _head_matmul(q, w_uk_t, NOPE_DIM).reshape(batch, H_C, KV_LORA)
                q_pe = q[:, QU_PE:].reshape(batch, H_C, ROPE_DIM)
                o_lat = _mla_decode(q_lat, q_pe, ckv.reshape(batch, 1, KV_LORA),
                                    kpe2[:, :ROPE_DIM].reshape(batch, 1, ROPE_DIM),
                                    past['ckv'], past['kpe'], past['page_table'], o)
                att = _head_matmul(o_lat.reshape(batch, H_C * KV_LORA), w_uv_h, KV_LORA)
            h = _resid_matmul(h, [att], [wts['w_out_mla'][o]])
            new['ckv'].append(ckv.reshape(batch, seq, KV_LORA))
            new['kpe'].append(kpe2[:, :ROPE_DIM].reshape(batch, seq, ROPE_DIM))
        h = _ffn(h, wts['norm_ffn'][i], wts['w_ffn_gate'][i], wts['w_ffn_up'][i], wts['w_ffn_down'][i])
        h = _ple(h, ple[i].reshape(m, -1), wts['norm_ple'][i], wts['w_ple_gate'][i], wts['w_ple_proj'][i],
                 wts['norm_final'], final=(i == depth - 1))
    return h.reshape(batch, seq, d), {name: jnp.stack(rows) for name, rows in new.items()}


def kernel(x_prompt, x_sample, p_prompt, p_sample, cache_sb_k, cache_sb_v, cache_mla_ckv, cache_mla_kpe,
           state_mlstm_c, state_mlstm_n, state_mlstm_m, page_table, norm_mix, norm_ffn, norm_ple, norm_final,
           w_in_ab, b_gates_ab, g_mlstm_head, w_out_ab, w_in_mla, g_q_lora, g_kv_lora, w_q_up, w_kv_up,
           w_out_mla, w_ffn_gate, w_ffn_up, w_ffn_down, w_ple_gate, w_ple_proj):
    wts = {
        'norm_mix': norm_mix, 'norm_ffn': norm_ffn, 'norm_ple': norm_ple, 'norm_final': norm_final,
        'b_gates_ab': b_gates_ab, 'g_mlstm_head': g_mlstm_head, 'g_q_lora': g_q_lora, 'g_kv_lora': g_kv_lora,
        'ab': [_prep_ab(w_in_ab[e], w_out_ab[e]) for e in range(w_in_ab.shape[0])],
        'mla': [_prep_mla(w_in_mla[o], w_q_up[o], w_kv_up[o]) for o in range(w_in_mla.shape[0])],
        'w_out_mla': w_out_mla.astype(BF16),
        'w_ffn_gate': w_ffn_gate.astype(BF16), 'w_ffn_up': w_ffn_up.astype(BF16),
        'w_ffn_down': w_ffn_down.astype(BF16),
        'w_ple_gate': w_ple_gate.astype(BF16), 'w_ple_proj': w_ple_proj.astype(BF16),
    }
    pos_prompt = jnp.arange(x_prompt.shape[1], dtype=jnp.int32)
    y_prompt, sp = _trunk(x_prompt, p_prompt, pos_prompt, wts, None)
    past_len = page_table.shape[1] * cache_sb_k.shape[2]
    past = {'sb_k': cache_sb_k, 'sb_v': cache_sb_v, 'ckv': cache_mla_ckv, 'kpe': jnp.swapaxes(cache_mla_kpe, 2, 3),
            'c': state_mlstm_c, 'n': state_mlstm_n, 'm': state_mlstm_m, 'page_table': page_table}
    pos_sample = past_len + jnp.arange(x_sample.shape[1], dtype=jnp.int32)
    y_sample, ss = _trunk(x_sample, p_sample, pos_sample, wts, past)
    return (y_prompt, y_sample,
            sp['sb_k'], sp['sb_v'], sp['c'], sp['n'], sp['m'], sp['ckv'], sp['kpe'],
            ss['sb_k'], ss['sb_v'], ss['c'], ss['n'], ss['m'], ss['ckv'], ss['kpe'])
```

```python
import functools

import jax
import jax.numpy as jnp
from jax import lax
from jax.experimental import pallas as pl
from jax.experimental.pallas import tpu as pltpu

F32 = jnp.float32
BF16 = jnp.bfloat16

EPS = 1e-6
H_A, DQK_A, DV_A, MLSTM_CHUNK = 4, 128, 256, 64
H_B, D_B = 8, 128
H_C, Q_LORA, KV_LORA, NOPE_DIM, ROPE_DIM, V_C = 16, 512, 512, 128, 64, 128
ROPE_THETA = 10000.0
SB_SCALE = D_B ** -0.5
MLA_SCALE = (NOPE_DIM + ROPE_DIM) ** -0.5
LANES = 128
SB_DONE = -104.0

AB_QA, AB_KA, AB_VA, AB_OA = 0, 512, 1024, 2048
AB_QB, AB_KB, AB_VB, AB_GATES = 3072, 4096, 5120, 6144
AB_WIDTH = AB_GATES + H_A * LANES
C_CQ, C_CKV, C_KPE, C_KROT = 0, 512, 1024, 1152
C_WIDTH = 1280
QU_NOPE, QU_PE, QU_ROT = 0, H_C * NOPE_DIM, H_C * NOPE_DIM + H_C * ROPE_DIM
QU_WIDTH = QU_ROT + H_C * ROPE_DIM
Q_WIDTH = QU_ROT

VMEM_LIMIT = 56 * 1024 * 1024


def _cparams(*sem):
    return pltpu.CompilerParams(dimension_semantics=sem, vmem_limit_bytes=VMEM_LIMIT)


def _dot(a, b):
    return jnp.dot(a, b, preferred_element_type=F32)


def _dot_nt(a, b):
    return lax.dot_general(a, b, (((1,), (1,)), ((), ())), preferred_element_type=F32)


def _dot_tn(a, b):
    return lax.dot_general(a, b, (((0,), (0,)), ((), ())), preferred_element_type=F32)


def _rms(x, g):
    return x * lax.rsqrt(jnp.mean(x * x, axis=-1, keepdims=True) + EPS) * g


def _softplus_neg_abs(z):
    return jnp.log(1.0 + jnp.exp(-jnp.abs(z)))


def _log_sigmoid(z):
    return jnp.minimum(z, 0.0) - _softplus_neg_abs(z)


def _sigmoid(z):
    return 1.0 / (1.0 + jnp.exp(-z))


def _tile(n, pref, align=8):
    if n <= pref:
        return n
    for t in range(pref - pref % align, 0, -align):
        if n % t == 0:
            return t
    raise ValueError((n, pref, align))


def _norm_matmul_body(x_ref, g_ref, w_ref, o_ref, xn_ref):
    @pl.when(pl.program_id(1) == 0)
    def _():
        xn_ref[...] = _rms(x_ref[...], g_ref[...]).astype(BF16)

    o_ref[...] = _dot(xn_ref[...], w_ref[...]).astype(o_ref.dtype)


def _norm_matmul(x, g, w, out_dtype, tm=512, tn=640):
    m, k = x.shape
    n = w.shape[1]
    tm, tn = _tile(m, tm), _tile(n, tn, LANES)
    return pl.pallas_call(
        _norm_matmul_body,
        grid=(m // tm, n // tn),
        in_specs=[pl.BlockSpec((tm, k), lambda i, j: (i, 0)),
                  pl.BlockSpec((1, k), lambda i, j: (0, 0)),
                  pl.BlockSpec((k, tn), lambda i, j: (0, j))],
        out_specs=pl.BlockSpec((tm, tn), lambda i, j: (i, j)),
        out_shape=jax.ShapeDtypeStruct((m, n), out_dtype),
        scratch_shapes=[pltpu.VMEM((tm, k), BF16)],
        compiler_params=_cparams("parallel", "arbitrary"),
        name="norm_matmul",
    )(x, g.reshape(1, k), w)


def _resid_matmul_body(*refs, n_in):
    h_ref, o_ref = refs[0], refs[-1]
    acc = h_ref[...]
    for i in range(n_in):
        acc = acc + _dot(refs[1 + 2 * i][...].astype(BF16), refs[2 + 2 * i][...])
    o_ref[...] = acc


def _resid_matmul(h, xs, ws, tm=512, tn=1024):
    m, n = h.shape
    tm, tn = _tile(m, tm), _tile(n, tn, LANES)
    in_specs = [pl.BlockSpec((tm, tn), lambda i, j: (i, j))]
    args = [h]
    for x, w in zip(xs, ws):
        k = x.shape[1]
        in_specs += [pl.BlockSpec((tm, k), lambda i, j: (i, 0)),
                     pl.BlockSpec((k, tn), lambda i, j: (0, j))]
        args += [x, w]
    return pl.pallas_call(
        functools.partial(_resid_matmul_body, n_in=len(xs)),
        grid=(m // tm, n // tn),
        in_specs=in_specs,
        out_specs=pl.BlockSpec((tm, tn), lambda i, j: (i, j)),
        out_shape=jax.ShapeDtypeStruct((m, n), F32),
        compiler_params=_cparams("parallel", "arbitrary"),
        name="resid_matmul",
    )(*args)


def _ffn_body(h_ref, g_ref, wg_ref, wu_ref, wd_ref, o_ref, xn_ref):
    @pl.when(pl.program_id(1) == 0)
    def _():
        h = h_ref[...]
        xn_ref[...] = _rms(h, g_ref[...]).astype(BF16)
        o_ref[...] = h

    xn = xn_ref[...]
    a = _dot(xn, wg_ref[...])
    u = _dot(xn, wu_ref[...])
    act = (a * _sigmoid(a) * u).astype(BF16)
    o_ref[...] += _dot(act, wd_ref[...])


def _ffn(h, g, wg, wu, wd, tm=512, tf=512):
    m, d = h.shape
    f = wg.shape[1]
    tm, tf = _tile(m, tm), _tile(f, tf, LANES)
    return pl.pallas_call(
        _ffn_body,
        grid=(m // tm, f // tf),
        in_specs=[pl.BlockSpec((tm, d), lambda i, j: (i, 0)),
                  pl.BlockSpec((1, d), lambda i, j: (0, 0)),
                  pl.BlockSpec((d, tf), lambda i, j: (0, j)),
                  pl.BlockSpec((d, tf), lambda i, j: (0, j)),
                  pl.BlockSpec((tf, d), lambda i, j: (j, 0))],
        out_specs=pl.BlockSpec((tm, d), lambda i, j: (i, 0)),
        out_shape=jax.ShapeDtypeStruct((m, d), F32),
        scratch_shapes=[pltpu.VMEM((tm, d), BF16)],
        compiler_params=_cparams("parallel", "arbitrary"),
        name="ffn",
    )(h, g.reshape(1, d), wg, wu, wd)


def _ple_body(h_ref, p_ref, g_ref, wg_ref, wp_ref, gf_ref, o_ref, *, final):
    h = h_ref[...]
    xn = _rms(h, g_ref[...]).astype(BF16)
    gate = _sigmoid(_dot(xn, wg_ref[...]))
    out = h + gate * _dot(p_ref[...].astype(BF16), wp_ref[...])
    if final:
        out = _rms(out, gf_ref[...])
    o_ref[...] = out


def _ple(h, p, g, wg, wp, g_final, final, tm=256):
    m, d = h.shape
    pd = p.shape[1]
    tm = _tile(m, tm)
    return pl.pallas_call(
        functools.partial(_ple_body, final=final),
        grid=(m // tm,),
        in_specs=[pl.BlockSpec((tm, d), lambda i: (i, 0)),
                  pl.BlockSpec((tm, pd), lambda i: (i, 0)),
                  pl.BlockSpec((1, d), lambda i: (0, 0)),
                  pl.BlockSpec((d, d), lambda i: (0, 0)),
                  pl.BlockSpec((pd, d), lambda i: (0, 0)),
                  pl.BlockSpec((1, d), lambda i: (0, 0))],
        out_specs=pl.BlockSpec((tm, d), lambda i: (i, 0)),
        out_shape=jax.ShapeDtypeStruct((m, d), F32),
        compiler_params=_cparams("parallel"),
        name="ple",
    )(h, p, g.reshape(1, d), wg, wp, g_final.reshape(1, d))


def _mlstm_prompt_body(bias_ref, q_ref, k_ref, v_ref, oa_ref, gc_ref, gr_ref, gh_ref,
                       ha_ref, c_ref, n_ref, m_ref, cs, ns, ms, *, tb, chunk):
    h = pl.program_id(1)
    t = pl.program_id(2)

    @pl.when(t == 0)
    def _():
        cs[...] = jnp.zeros_like(cs)
        ns[...] = jnp.zeros_like(ns)
        ms[...] = jnp.zeros_like(ms)

    b_ig = bias_ref[h]
    b_fg = bias_ref[H_A + h]
    lane = lax.broadcasted_iota(jnp.int32, (tb, LANES), 1)
    gcb = gc_ref[...] + jnp.where(lane == 0, b_ig, b_fg)
    lsg_c = _log_sigmoid(gcb)
    row = lax.broadcasted_iota(jnp.int32, (8, tb), 0)
    grb = gr_ref[...] + jnp.where(row == 0, b_ig, b_fg)
    lsg_r = _log_sigmoid(grb)
    r = lax.broadcasted_iota(jnp.int32, (tb, tb), 0)
    c = lax.broadcasted_iota(jnp.int32, (tb, tb), 1)
    shift = chunk.bit_length() - 1
    same = jnp.right_shift(r, shift) == jnp.right_shift(c, shift)
    tril = jnp.where(same, jnp.where(r >= c, 1.0, 0.0), 0.0).astype(F32)
    bc_all = jnp.dot(tril, lsg_c, precision=lax.Precision.HIGHEST, preferred_element_type=F32)
    br_all = lax.dot_general(lsg_r, tril, (((1,), (1,)), ((), ())),
                             precision=lax.Precision.HIGHEST, preferred_element_type=F32)
    tt = lax.broadcasted_iota(jnp.int32, (chunk, chunk), 0)
    ss = lax.broadcasted_iota(jnp.int32, (chunk, chunk), 1)
    causal = ss <= tt
    gh = gh_ref[...]

    for ci in range(tb // chunk):
        lo, hi = ci * chunk, (ci + 1) * chunk
        bcol = bc_all[lo:hi, 1:2]
        igcol = gcb[lo:hi, 0:1]
        brow = br_all[1:2, lo:hi]
        igrow = grb[0:1, lo:hi]
        m_prev = ms[...]
        d = jnp.where(causal, bcol - brow + igrow, -jnp.inf)
        g = bcol + m_prev
        m_t = jnp.maximum(g, jnp.max(d, axis=-1, keepdims=True))
        w_in = jnp.exp(g - m_t)
        qf = q_ref[lo:hi, :]
        qb = qf.astype(BF16)
        kf = k_ref[lo:hi, :] * (DQK_A ** -0.5)
        kb = kf.astype(BF16)
        vb = v_ref[lo:hi, :].astype(BF16)
        s = _dot_nt(qb, kb) * jnp.exp(d - m_t)
        cmat = cs[...]
        nvec = ns[...]
        num = w_in * _dot(qb, cmat.astype(BF16)) + _dot(s.astype(BF16), vb)
        den = w_in * jnp.sum(qf * nvec, axis=-1, keepdims=True) + jnp.sum(s, axis=-1, keepdims=True)
        hh = num / jnp.maximum(jnp.abs(den), jnp.exp(-m_t))
        m_new = m_t[chunk - 1:chunk, :]
        decay = jnp.exp(g[chunk - 1:chunk, :] - m_new)
        wk = jnp.exp(bcol[chunk - 1:chunk, :] - bcol + igcol - m_new)
        kw = kf * wk
        cs[...] = decay * cmat + _dot_tn(kw.astype(BF16), vb)
        ns[...] = decay * nvec + jnp.sum(kw, axis=0, keepdims=True)
        ms[...] = m_new
        hn = _rms(hh, gh)
        ha_ref[lo:hi, :] = (hn * _sigmoid(oa_ref[lo:hi, :])).astype(ha_ref.dtype)

    @pl.when(t == pl.num_programs(2) - 1)
    def _():
        c_ref[...] = cs[...]
        n_ref[...] = ns[...]
        m_ref[...] = ms[...]


def _mlstm_prompt(proj, gates_t, b_gates, g_head, batch, seq):
    chunk = MLSTM_CHUNK if seq % MLSTM_CHUNK == 0 else seq
    tb = _tile(seq, 512) if seq % MLSTM_CHUNK == 0 else seq
    nt = seq // tb
    m = batch * seq
    qo, ko, vo, oo, go = (AB_QA // DQK_A, AB_KA // DQK_A, AB_VA // DV_A, AB_OA // DV_A, AB_GATES // LANES)
    return pl.pallas_call(
        functools.partial(_mlstm_prompt_body, tb=tb, chunk=chunk),
        grid=(batch, H_A, nt),
        in_specs=[pl.BlockSpec(memory_space=pltpu.SMEM),
                  pl.BlockSpec((tb, DQK_A), lambda b, h, t: (b * nt + t, qo + h)),
                  pl.BlockSpec((tb, DQK_A), lambda b, h, t: (b * nt + t, ko + h)),
                  pl.BlockSpec((tb, DV_A), lambda b, h, t: (b * nt + t, vo + h)),
                  pl.BlockSpec((tb, DV_A), lambda b, h, t: (b * nt + t, oo + h)),
                  pl.BlockSpec((tb, LANES), lambda b, h, t: (b * nt + t, go + h)),
                  pl.BlockSpec((None, 8, tb), lambda b, h, t: (h, 0, b * nt + t)),
                  pl.BlockSpec((1, DV_A), lambda b, h, t: (0, h))],
        out_specs=[pl.BlockSpec((tb, DV_A), lambda b, h, t: (b * nt + t, h)),
                   pl.BlockSpec((None, None, DQK_A, DV_A), lambda b, h, t: (b, h, 0, 0)),
                   pl.BlockSpec((None, None, 1, DQK_A), lambda b, h, t: (b, h, 0, 0)),
                   pl.BlockSpec((None, None, 1, 1), lambda b, h, t: (b, h, 0, 0))],
        out_shape=[jax.ShapeDtypeStruct((m, H_A * DV_A), BF16),
                   jax.ShapeDtypeStruct((batch, H_A, DQK_A, DV_A), F32),
                   jax.ShapeDtypeStruct((batch, H_A, 1, DQK_A), F32),
                   jax.ShapeDtypeStruct((batch, H_A, 1, 1), F32)],
        scratch_shapes=[pltpu.VMEM((DQK_A, DV_A), F32), pltpu.VMEM((1, DQK_A), F32), pltpu.VMEM((1, 1), F32)],
        compiler_params=_cparams("parallel", "parallel", "arbitrary"),
        name="mlstm_prompt",
    )(b_gates, proj, proj, proj, proj, proj, gates_t, g_head.reshape(1, H_A * DV_A))


def _mlstm_decode_body(bias_ref, q_ref, k_ref, v_ref, oa_ref, gc_ref, gh_ref, c0_ref, n0_ref, m0_ref,
                       ha_ref, c_ref, n_ref, m_ref, *, tb):
    h = pl.program_id(1)
    b_ig = bias_ref[h]
    b_fg = bias_ref[H_A + h]
    gc = gc_ref[...]
    ig = gc[:, 0:1] + b_ig
    fg = _log_sigmoid(gc[:, 1:2] + b_fg)
    m0 = m0_ref[...]
    g = fg + m0
    m_t = jnp.maximum(g, ig)
    w_in = jnp.exp(g - m_t)
    e_ig = jnp.exp(ig - m_t)
    qf = q_ref[...]
    kf = k_ref[...] * (DQK_A ** -0.5)
    vf = v_ref[...]
    vb = vf.astype(BF16)
    qk = jnp.sum(qf.astype(BF16).astype(F32) * kf.astype(BF16).astype(F32), axis=-1, keepdims=True)
    s = qk * e_ig
    kw = kf * e_ig
    rows = lax.broadcasted_iota(jnp.int32, (tb, 1), 0)

    def step(b, hq):
        sel = rows == b
        cb = c0_ref[b]
        hq = hq + _dot(jnp.where(sel, qf, 0.0).astype(BF16), cb.astype(BF16))
        upd = _dot_tn(jnp.where(sel, kw, 0.0).astype(BF16), vb)
        decay = jnp.sum(jnp.where(sel, w_in, 0.0), axis=0, keepdims=True)
        c_ref[b] = decay * cb + upd
        return hq

    hq = lax.fori_loop(0, tb, step, jnp.zeros((tb, DV_A), F32))
    n0 = n0_ref[...]
    num = w_in * hq + s * vb.astype(F32)
    den = w_in * jnp.sum(qf * n0, axis=-1, keepdims=True) + s
    hh = num / jnp.maximum(jnp.abs(den), jnp.exp(-m_t))
    n_ref[...] = w_in * n0 + kw
    m_ref[...] = m_t
    hn = _rms(hh, gh_ref[...])
    ha_ref[...] = (hn * _sigmoid(oa_ref[...])).astype(ha_ref.dtype)


def _mlstm_decode(proj, b_gates, g_head, c0, n0, m0, layer):
    batch = proj.shape[0]
    tb = _tile(batch, 32)
    qo, ko, vo, oo, go = (AB_QA // DQK_A, AB_KA // DQK_A, AB_VA // DV_A, AB_OA // DV_A, AB_GATES // LANES)
    return pl.pallas_call(
        functools.partial(_mlstm_decode_body, tb=tb),
        grid=(batch // tb, H_A),
        in_specs=[pl.BlockSpec(memory_space=pltpu.SMEM),
                  pl.BlockSpec((tb, DQK_A), lambda i, h: (i, qo + h)),
                  pl.BlockSpec((tb, DQK_A), lambda i, h: (i, ko + h)),
                  pl.BlockSpec((tb, DV_A), lambda i, h: (i, vo + h)),
                  pl.BlockSpec((tb, DV_A), lambda i, h: (i, oo + h)),
                  pl.BlockSpec((tb, LANES), lambda i, h: (i, go + h)),
                  pl.BlockSpec((1, DV_A), lambda i, h: (0, h)),
                  pl.BlockSpec((None, tb, None, DQK_A, DV_A), lambda i, h: (layer, i, h, 0, 0)),
                  pl.BlockSpec((tb, DQK_A), lambda i, h: (i, h)),
                  pl.BlockSpec((None, tb, 1), lambda i, h: (h, i, 0))],
        out_specs=[pl.BlockSpec((tb, DV_A), lambda i, h: (i, h)),
                   pl.BlockSpec((tb, None, DQK_A, DV_A), lambda i, h: (i, h, 0, 0)),
                   pl.BlockSpec((tb, DQK_A), lambda i, h: (i, h)),
                   pl.BlockSpec((None, tb, 1), lambda i, h: (h, i, 0))],
        out_shape=[jax.ShapeDtypeStruct((batch, H_A * DV_A), BF16),
                   jax.ShapeDtypeStruct((batch, H_A, DQK_A, DV_A), F32),
                   jax.ShapeDtypeStruct((batch, H_A * DQK_A), F32),
                   jax.ShapeDtypeStruct((H_A, batch, 1), F32)],
        compiler_params=_cparams("parallel", "parallel"),
        name="mlstm_decode",
    )(b_gates, proj, proj, proj, proj, proj, g_head.reshape(1, H_A * DV_A), c0, n0, m0)


def _sb_weights(z, mask, tri, carry):
    sp = _softplus_neg_abs(z)
    ls = jnp.minimum(z, 0.0) - sp
    lk = -(jnp.maximum(z, 0.0) + sp)
    if mask is not None:
        lk = jnp.where(mask, lk, 0.0)
    hi = lk.astype(BF16)
    lo = (lk - hi.astype(F32)).astype(BF16)
    la = _dot(hi, tri) + _dot(lo, tri) + carry
    a = jnp.exp(ls + la)
    if mask is not None:
        a = jnp.where(mask, a, 0.0)
    return a, carry + jnp.sum(lk, axis=-1, keepdims=True)


def _later_key_matrix(n):
    r = lax.broadcasted_iota(jnp.int32, (n, n), 0)
    c = lax.broadcasted_iota(jnp.int32, (n, n), 1)
    return jnp.where(r > c, 1.0, 0.0).astype(BF16)


def _sb_prompt_body(q_ref, k_ref, v_ref, o_ref, acc_ref, carry_ref, done_ref, *, tq):
    i = pl.program_id(1)
    jj = pl.program_id(2)
    j = i - jj

    @pl.when(jj == 0)
    def _():
        acc_ref[...] = jnp.zeros_like(acc_ref)
        carry_ref[...] = jnp.zeros_like(carry_ref)
        done_ref[0] = 0

    @pl.when(jnp.logical_and(j >= 0, done_ref[0] == 0))
    def _():
        qpos = i * tq + lax.broadcasted_iota(jnp.int32, (tq, tq), 0)
        kpos = j * tq + lax.broadcasted_iota(jnp.int32, (tq, tq), 1)
        mask = kpos < qpos
        tri = _later_key_matrix(tq)
        for h in range(H_B):
            sl = slice(h * D_B, (h + 1) * D_B)
            z = _dot_nt(q_ref[:, sl].astype(BF16), k_ref[:, sl].astype(BF16)) * SB_SCALE
            a, carry = _sb_weights(z, mask, tri, carry_ref[h])
            acc_ref[h] += _dot(a.astype(BF16), v_ref[:, sl].astype(BF16))
            carry_ref[h] = carry
        done_ref[0] = (jnp.max(carry_ref[...]) < SB_DONE).astype(jnp.int32)

    @pl.when(jj == pl.num_programs(2) - 1)
    def _():
        for h in range(H_B):
            o_ref[:, h * D_B:(h + 1) * D_B] = acc_ref[h].astype(o_ref.dtype)


def _sb_prompt(proj, batch, seq):
    tq = _tile(seq, 256)
    nq = seq // tq
    width = H_B * D_B
    qo, ko, vo = AB_QB // width, AB_KB // width, AB_VB // width
    return pl.pallas_call(
        functools.partial(_sb_prompt_body, tq=tq),
        grid=(batch, nq, nq),
        in_specs=[pl.BlockSpec((tq, width), lambda b, i, jj: (b * nq + i, qo)),
                  pl.BlockSpec((tq, width), lambda b, i, jj: (b * nq + jnp.maximum(i - jj, 0), ko)),
                  pl.BlockSpec((tq, width), lambda b, i, jj: (b * nq + jnp.maximum(i - jj, 0), vo))],
        out_specs=pl.BlockSpec((tq, width), lambda b, i, jj: (b * nq + i, 0)),
        out_shape=jax.ShapeDtypeStruct((batch * seq, width), BF16),
        scratch_shapes=[pltpu.VMEM((H_B, tq, D_B), F32), pltpu.VMEM((H_B, tq, 1), F32),
                        pltpu.SMEM((1,), jnp.int32)],
        compiler_params=_cparams("parallel", "parallel", "arbitrary"),
        name="sb_prompt",
    )(proj, proj, proj)


def _sb_decode_body(pt_ref, q_ref, k_hbm, v_hbm, o_ref, fk, fv, kb, vb, fsem, bsem, acc_ref, carry_ref,
                    *, n_pages, page, layer):
    b = pl.program_id(0)
    nb = pl.num_programs(0)

    def head_copies(src, dst, sem):
        return [pltpu.make_async_copy(src.at[:, h, :], dst.at[h], sem) for h in range(H_B)]

    def first_copies(seq, slot):
        pg = pt_ref[seq, n_pages - 1]
        return (head_copies(k_hbm.at[layer, pg], fk.at[slot], fsem.at[0, slot])
                + head_copies(v_hbm.at[layer, pg], fv.at[slot], fsem.at[1, slot]))

    def page_copies(p, slot):
        pg = pt_ref[b, p]
        return (head_copies(k_hbm.at[layer, pg], kb.at[slot], bsem.at[0, slot])
                + head_copies(v_hbm.at[layer, pg], vb.at[slot], bsem.at[1, slot]))

    @pl.when(b == 0)
    def _():
        for c in first_copies(0, 0):
            c.start()

    @pl.when(b + 1 < nb)
    def _():
        for c in first_copies(b + 1, (b + 1) % 2):
            c.start()

    if n_pages > 1:
        for c in page_copies(n_pages - 2, 0):
            c.start()

    acc_ref[...] = jnp.zeros_like(acc_ref)
    carry_ref[...] = jnp.zeros_like(carry_ref)
    q = q_ref[...]
    qall = jnp.concatenate([q[:, h * D_B:(h + 1) * D_B] for h in range(H_B)], axis=0).astype(BF16)
    head = lax.broadcasted_iota(jnp.int32, (H_B, page), 0)
    head_d = lax.broadcasted_iota(jnp.int32, (H_B, D_B), 0)
    tri = _later_key_matrix(page)

    def sweep_page(k_ref, v_ref):
        z = jnp.zeros((H_B, page), F32)
        for h in range(H_B):
            z = jnp.where(head == h, _dot_nt(qall, k_ref[h].astype(BF16)), z)
        a, carry = _sb_weights(z * SB_SCALE, None, tri, carry_ref[...])
        ab = a.astype(BF16)
        upd = jnp.zeros((H_B, D_B), F32)
        for h in range(H_B):
            upd = jnp.where(head_d == h, _dot(ab, v_ref[h].astype(BF16)), upd)
        acc_ref[...] += upd
        carry_ref[...] = carry
        return jnp.max(carry) < SB_DONE

    fslot = b % 2
    for c in first_copies(b, fslot):
        c.wait()
    done0 = sweep_page(fk.at[fslot], fv.at[fslot])

    def more(st):
        p, _, done = st
        return jnp.logical_and(p >= 0, jnp.logical_not(done))

    def step(st):
        p, slot, _ = st
        for c in page_copies(p, slot):
            c.wait()

        @pl.when(p >= 1)
        def _():
            for c in page_copies(p - 1, 1 - slot):
                c.start()

        return p - 1, 1 - slot, sweep_page(kb.at[slot], vb.at[slot])

    p, slot, _ = lax.while_loop(more, step, (jnp.int32(n_pages - 2), jnp.int32(0), done0))

    @pl.when(p >= 0)
    def _():
        for c in page_copies(p, slot):
            c.wait()

    for h in range(H_B):
        o_ref[:, h * D_B:(h + 1) * D_B] = acc_ref[h:h + 1, :].astype(o_ref.dtype)


def _sb_decode(q, cache_k, cache_v, page_table, layer):
    batch = q.shape[0]
    n_pages = page_table.shape[1]
    page = cache_k.shape[2]
    page_buf = pltpu.VMEM((2, H_B, page, D_B), cache_k.dtype)
    grid_spec = pltpu.PrefetchScalarGridSpec(
        num_scalar_prefetch=1,
        grid=(batch,),
        in_specs=[pl.BlockSpec((None, 1, H_B * D_B), lambda b, pt: (b, 0, 0)),
                  pl.BlockSpec(memory_space=pl.ANY),
                  pl.BlockSpec(memory_space=pl.ANY)],
        out_specs=pl.BlockSpec((None, 1, H_B * D_B), lambda b, pt: (b, 0, 0)),
        scratch_shapes=[page_buf, page_buf, page_buf, page_buf,
                        pltpu.SemaphoreType.DMA((2, 2)), pltpu.SemaphoreType.DMA((2, 2)),
                        pltpu.VMEM((H_B, D_B), F32), pltpu.VMEM((H_B, 1), F32)],
    )
    return pl.pallas_call(
        functools.partial(_sb_decode_body, n_pages=n_pages, page=page, layer=layer),
        grid_spec=grid_spec,
        out_shape=jax.ShapeDtypeStruct((batch, 1, H_B * D_B), BF16),
        compiler_params=_cparams("arbitrary"),
        name="sb_decode",
    )(page_table, q, cache_k, cache_v)


def _mla_up_body(pr_ref, gq_ref, gkv_ref, wq_ref, wk_ref, wvt_ref, cos_ref, sin_ref, *out_refs, with_kv):
    q_ref, ckv_ref, kpe_ref = out_refs[:3]
    cos = cos_ref[...]
    sin = sin_ref[...]
    cq = _rms(pr_ref[:, C_CQ:C_CQ + Q_LORA], gq_ref[...]).astype(BF16)
    ckv = _rms(pr_ref[:, C_CKV:C_CKV + KV_LORA], gkv_ref[...])
    ckv_ref[...] = ckv
    kpe_ref[...] = pr_ref[:, C_KPE:C_KPE + LANES] * cos + pr_ref[:, C_KROT:C_KROT + LANES] * sin
    qa = _dot(cq, wq_ref[...])
    q_ref[:, QU_NOPE:QU_PE] = qa[:, QU_NOPE:QU_PE].astype(BF16)
    for c in range((QU_ROT - QU_PE) // LANES):
        pe = qa[:, QU_PE + c * LANES:QU_PE + (c + 1) * LANES]
        rot = qa[:, QU_ROT + c * LANES:QU_ROT + (c + 1) * LANES]
        q_ref[:, QU_PE + c * LANES:QU_PE + (c + 1) * LANES] = (pe * cos + rot * sin).astype(BF16)
    if with_kv:
        ckv_b = ckv.astype(BF16)
        out_refs[3][...] = _dot(ckv_b, wk_ref[...]).astype(BF16)
        out_refs[4][...] = _dot_nt(wvt_ref[...], ckv_b).astype(BF16)


def _mla_up(proj, g_q, g_kv, w_q, w_k, w_vt, cos, sin, with_kv, tm=256):
    m = proj.shape[0]
    tm = _tile(m, tm)
    k_w, v_w = w_k.shape[1], w_vt.shape[0]
    out_specs = [pl.BlockSpec((tm, Q_WIDTH), lambda i: (i, 0)),
                 pl.BlockSpec((tm, KV_LORA), lambda i: (i, 0)),
                 pl.BlockSpec((tm, LANES), lambda i: (i, 0))]
    out_shape = [jax.ShapeDtypeStruct((m, Q_WIDTH), BF16),
                 jax.ShapeDtypeStruct((m, KV_LORA), F32),
                 jax.ShapeDtypeStruct((m, LANES), F32)]
    if with_kv:
        out_specs += [pl.BlockSpec((tm, k_w), lambda i: (i, 0)), pl.BlockSpec((v_w, tm), lambda i: (0, i))]
        out_shape += [jax.ShapeDtypeStruct((m, k_w), BF16), jax.ShapeDtypeStruct((v_w, m), BF16)]
    return pl.pallas_call(
        functools.partial(_mla_up_body, with_kv=with_kv),
        grid=(m // tm,),
        in_specs=[pl.BlockSpec((tm, C_WIDTH), lambda i: (i, 0)),
                  pl.BlockSpec((1, Q_LORA), lambda i: (0, 0)),
                  pl.BlockSpec((1, KV_LORA), lambda i: (0, 0)),
                  pl.BlockSpec((Q_LORA, QU_WIDTH), lambda i: (0, 0)),
                  pl.BlockSpec((KV_LORA, k_w), lambda i: (0, 0)),
                  pl.BlockSpec((v_w, KV_LORA), lambda i: (0, 0)),
                  pl.BlockSpec((tm, LANES), lambda i: (i, 0)),
                  pl.BlockSpec((tm, LANES), lambda i: (i, 0))],
        out_specs=out_specs,
        out_shape=out_shape,
        compiler_params=_cparams("parallel"),
        name="mla_up",
    )(proj, g_q.reshape(1, Q_LORA), g_kv.reshape(1, KV_LORA), w_q, w_k, w_vt, cos, sin)


def _mla_prompt_body(q_ref, kn_ref, vt_ref, kpe_ref, o_ref, m_ref, l_ref, acc_ref, *, tq):
    i = pl.program_id(1)
    j = pl.program_id(2)

    @pl.when(j == 0)
    def _():
        m_ref[...] = jnp.full_like(m_ref, -1e30)
        l_ref[...] = jnp.zeros_like(l_ref)
        acc_ref[...] = jnp.zeros_like(acc_ref)

    @pl.when(j <= i)
    def _():
        kpos = j * tq + lax.broadcasted_iota(jnp.int32, (tq, tq), 0)
        qpos = i * tq + lax.broadcasted_iota(jnp.int32, (tq, tq), 1)
        mask = kpos <= qpos
        kpe = kpe_ref[...]
        lane = lax.broadcasted_iota(jnp.int32, kpe.shape, 1)
        kpe_sel = (jnp.where(lane < ROPE_DIM, kpe, 0.0).astype(BF16),
                   jnp.where(lane >= ROPE_DIM, kpe, 0.0).astype(BF16))
        for h in range(H_C):
            sl = slice(h * NOPE_DIM, (h + 1) * NOPE_DIM)
            pe = slice(QU_PE + (h // 2) * LANES, QU_PE + (h // 2 + 1) * LANES)
            qcat = jnp.concatenate([q_ref[:, sl], q_ref[:, pe]], axis=1)
            kcat = jnp.concatenate([kn_ref[:, sl], kpe_sel[h % 2]], axis=1)
            s = jnp.where(mask, _dot_nt(kcat, qcat) * MLA_SCALE, -jnp.inf)
            m_prev = m_ref[h]
            m_new = jnp.maximum(m_prev, jnp.max(s, axis=0, keepdims=True))
            alpha = jnp.exp(m_prev - m_new)
            p = jnp.exp(s - m_new)
            l_ref[h] = alpha * l_ref[h] + jnp.sum(p, axis=0, keepdims=True)
            acc_ref[h] = alpha * acc_ref[h] + _dot(vt_ref[sl, :], p.astype(BF16))
            m_ref[h] = m_new

    @pl.when(j == pl.num_programs(2) - 1)
    def _():
        for h in range(H_C):
            o_ref[:, h * V_C:(h + 1) * V_C] = jnp.transpose(acc_ref[h] / l_ref[h]).astype(o_ref.dtype)


def _mla_prompt(q, kn, vt, kpe2, batch, seq):
    tq = _tile(seq, 512)
    nq = seq // tq
    kw, vw = H_C * NOPE_DIM, H_C * V_C
    return pl.pallas_call(
        functools.partial(_mla_prompt_body, tq=tq),
        grid=(batch, nq, nq),
        in_specs=[pl.BlockSpec((tq, Q_WIDTH), lambda b, i, j: (b * nq + i, 0)),
                  pl.BlockSpec((tq, kw), lambda b, i, j: (b * nq + jnp.minimum(i, j), 0)),
                  pl.BlockSpec((vw, tq), lambda b, i, j: (0, b * nq + jnp.minimum(i, j))),
                  pl.BlockSpec((tq, LANES), lambda b, i, j: (b * nq + jnp.minimum(i, j), 0))],
        out_specs=pl.BlockSpec((tq, vw), lambda b, i, j: (b * nq + i, 0)),
        out_shape=jax.ShapeDtypeStruct((batch * seq, vw), BF16),
        scratch_shapes=[pltpu.VMEM((H_C, 1, tq), F32), pltpu.VMEM((H_C, 1, tq), F32),
                        pltpu.VMEM((H_C, V_C, tq), F32)],
        compiler_params=_cparams("parallel", "parallel", "arbitrary"),
        name="mla_prompt",
    )(q, kn, vt, kpe2)


def _head_matmul_body(x_ref, w_ref, o_ref):
    o_ref[...] = _dot(x_ref[...], w_ref[...]).astype(o_ref.dtype)


def _head_matmul(x, w, d_in, col0=0):
    m = x.shape[0]
    n_h, _, d_out = w.shape
    off = col0 // d_in
    return pl.pallas_call(
        _head_matmul_body,
        grid=(n_h,),
        in_specs=[pl.BlockSpec((m, d_in), lambda h: (0, off + h)),
                  pl.BlockSpec((None, d_in, d_out), lambda h: (h, 0, 0))],
        out_specs=pl.BlockSpec((m, d_out), lambda h: (0, h)),
        out_shape=jax.ShapeDtypeStruct((m, n_h * d_out), BF16),
        compiler_params=_cparams("parallel"),
        name="head_matmul",
    )(x, w)


def _mla_decode_body(pt_ref, ql_ref, qp_ref, cn_ref, kn_ref, *refs, pages_per_step, page):
    del pt_ref
    g_n = pages_per_step
    c_refs, p_refs = refs[:g_n], refs[g_n:2 * g_n]
    o_ref, kbuf, pbuf, m_ref, l_ref, acc_ref = refs[2 * g_n:]
    jj = pl.program_id(1)
    ql = ql_ref[...]
    qp = qp_ref[...]

    @pl.when(jj == 0)
    def _():
        lat = cn_ref[...].astype(BF16).astype(F32)
        kpe = kn_ref[...].astype(BF16).astype(F32)
        s0 = (jnp.sum(ql.astype(F32) * lat, axis=-1, keepdims=True)
              + jnp.sum(qp.astype(F32) * kpe, axis=-1, keepdims=True)) * MLA_SCALE
        m_ref[...] = s0
        l_ref[...] = jnp.ones_like(l_ref)
        acc_ref[...] = jnp.broadcast_to(lat, acc_ref.shape)

    for g in range(g_n):
        kbuf[g * page:(g + 1) * page, :] = c_refs[g][...].astype(BF16)
        pbuf[:, g * page:(g + 1) * page] = p_refs[g][...].astype(BF16)
    s = (_dot_nt(ql, kbuf[...]) + _dot(qp, pbuf[...])) * MLA_SCALE
    m_prev = m_ref[...]
    m_new = jnp.maximum(m_prev, jnp.max(s, axis=-1, keepdims=True))
    alpha = jnp.exp(m_prev - m_new)
    p = jnp.exp(s - m_new)
    l_ref[...] = alpha * l_ref[...] + jnp.sum(p, axis=-1, keepdims=True)
    acc_ref[...] = alpha * acc_ref[...] + _dot(p.astype(BF16), kbuf[...])
    m_ref[...] = m_new

    @pl.when(jj == pl.num_programs(1) - 1)
    def _():
        o_ref[...] = (acc_ref[...] / l_ref[...]).astype(o_ref.dtype)


def _mla_decode(q_lat, q_pe, ckv_new, kpe_new, cache_ckv, cache_kpe_t, page_table, layer):
    batch = q_lat.shape[0]
    n_pages = page_table.shape[1]
    page = cache_ckv.shape[2]
    g_n = _tile(n_pages, 32, 1)
    steps = n_pages // g_n

    def page_spec(rows, cols, g):
        return pl.BlockSpec((None, None, rows, cols), lambda b, jj, pt: (layer, pt[b, jj * g_n + g], 0, 0))

    grid_spec = pltpu.PrefetchScalarGridSpec(
        num_scalar_prefetch=1,
        grid=(batch, steps),
        in_specs=[pl.BlockSpec((None, H_C, KV_LORA), lambda b, jj, pt: (b, 0, 0)),
                  pl.BlockSpec((None, H_C, ROPE_DIM), lambda b, jj, pt: (b, 0, 0)),
                  pl.BlockSpec((None, 1, KV_LORA), lambda b, jj, pt: (b, 0, 0)),
                  pl.BlockSpec((None, 1, ROPE_DIM), lambda b, jj, pt: (b, 0, 0))]
        + [page_spec(page, KV_LORA, g) for g in range(g_n)]
        + [page_spec(ROPE_DIM, page, g) for g in range(g_n)],
        out_specs=pl.BlockSpec((None, H_C, KV_LORA), lambda b, jj, pt: (b, 0, 0)),
        scratch_shapes=[pltpu.VMEM((g_n * page, KV_LORA), BF16),
                        pltpu.VMEM((ROPE_DIM, g_n * page), BF16),
                        pltpu.VMEM((H_C, 1), F32), pltpu.VMEM((H_C, 1), F32),
                        pltpu.VMEM((H_C, KV_LORA), F32)],
    )
    return pl.pallas_call(
        functools.partial(_mla_decode_body, pages_per_step=g_n, page=page),
        grid_spec=grid_spec,
        out_shape=jax.ShapeDtypeStruct((batch, H_C, KV_LORA), BF16),
        compiler_params=_cparams("parallel", "arbitrary"),
        name="mla_decode",
    )(page_table, q_lat, q_pe, ckv_new, kpe_new, *([cache_ckv] * g_n), *([cache_kpe_t] * g_n))


def _rot_cols(w):
    k = w.shape[0]
    w4 = w.reshape(k, -1, 2, ROPE_DIM // 2)
    return jnp.stack([-w4[:, :, 1], w4[:, :, 0]], axis=2).reshape(k, -1)


def _prep_ab(w_in, w_out):
    d = w_in.shape[0]
    sizes = (H_A * DQK_A, H_A * DQK_A, H_A * DV_A, H_A * DV_A, 2 * H_A, H_B * D_B, H_B * D_B, H_B * D_B)
    parts, start = [], 0
    for sz in sizes:
        parts.append(w_in[:, start:start + sz])
        start += sz
    qa, ka, va, oa, gates, qb, kb, vb = parts
    gcols = jnp.zeros((d, H_A, LANES), w_in.dtype)
    gcols = gcols.at[:, :, 0].set(gates[:, :H_A]).at[:, :, 1].set(gates[:, H_A:])
    w_ext = jnp.concatenate([qa, ka, va, oa, qb, kb, vb, gcols.reshape(d, H_A * LANES)], axis=1).astype(BF16)
    w_out = w_out.astype(BF16)
    return w_ext, w_out[:H_A * DV_A], w_out[H_A * DV_A:]


def _prep_mla(w_in, w_q_up, w_kv_up):
    kpe_w = w_in[:, Q_LORA + KV_LORA:]
    rot_w = _rot_cols(kpe_w)
    w_in_ext = jnp.concatenate([w_in[:, :Q_LORA + KV_LORA], kpe_w, kpe_w, rot_w, rot_w], axis=1).astype(BF16)
    wq = w_q_up.reshape(Q_LORA, H_C, NOPE_DIM + ROPE_DIM)
    wq_pe = wq[:, :, NOPE_DIM:].reshape(Q_LORA, H_C * ROPE_DIM)
    w_q_ext = jnp.concatenate([wq[:, :, :NOPE_DIM].reshape(Q_LORA, H_C * NOPE_DIM), wq_pe, _rot_cols(wq_pe)],
                              axis=1).astype(BF16)
    wkv = w_kv_up.reshape(KV_LORA, H_C, NOPE_DIM + V_C)
    w_uk, w_uv = wkv[:, :, :NOPE_DIM], wkv[:, :, NOPE_DIM:]
    w_k = w_uk.reshape(KV_LORA, -1).astype(BF16)
    w_vt = jnp.transpose(w_uv.reshape(KV_LORA, -1)).astype(BF16)
    w_uk_t = jnp.transpose(w_uk, (1, 2, 0)).astype(BF16)
    w_uv_h = jnp.transpose(w_uv, (1, 0, 2)).astype(BF16)
    return w_in_ext, w_q_ext, w_k, w_vt, w_uk_t, w_uv_h


def _rope_tables(pos):
    half = ROPE_DIM // 2
    freqs = ROPE_THETA ** (-jnp.arange(half, dtype=F32) / half)
    ang = pos.astype(F32)[:, None] * freqs[None, :]
    reps = LANES // half
    return jnp.tile(jnp.cos(ang), (1, reps)), jnp.tile(jnp.sin(ang), (1, reps))


def _trunk(x, ple, pos, wts, past):
    batch, seq, d = x.shape
    m = batch * seq
    h = x.reshape(m, d)
    depth = wts['norm_mix'].shape[0]
    cos, sin = _rope_tables(pos)
    cos, sin = jnp.tile(cos, (batch, 1)), jnp.tile(sin, (batch, 1))
    new = {name: [] for name in ('sb_k', 'sb_v', 'c', 'n', 'm', 'ckv', 'kpe')}
    for i in range(depth):
        if i % 2 == 0:
            e = i // 2
            w_ext, w_out_a, w_out_b = wts['ab'][e]
            proj = _norm_matmul(h, wts['norm_mix'][i], w_ext, F32, tn=AB_WIDTH // 4)
            kb = proj[:, AB_KB:AB_KB + H_B * D_B]
            vb = proj[:, AB_VB:AB_VB + H_B * D_B]
            b_gates = wts['b_gates_ab'][e].astype(F32)
            g_head = wts['g_mlstm_head'][e].astype(F32)
            if past is None:
                gates_t = proj[:, AB_GATES:].reshape(m, H_A, LANES)[:, :, :8]
                gates_t = jnp.transpose(gates_t, (1, 2, 0))
                ha, c_new, n_new, m_new = _mlstm_prompt(proj, gates_t, b_gates, g_head, batch, seq)
                n_new = n_new.reshape(batch, H_A, DQK_A)
                m_new = m_new.reshape(batch, H_A)
                hb = _sb_prompt(proj, batch, seq)
            else:
                n0 = past['n'][e].astype(F32).reshape(batch, H_A * DQK_A)
                m0 = jnp.transpose(past['m'][e].astype(F32))[:, :, None]
                ha, c_new, n_new, m_new = _mlstm_decode(proj, b_gates, g_head, past['c'], n0, m0, e)
                n_new = n_new.reshape(batch, H_A, DQK_A)
                m_new = jnp.transpose(m_new[:, :, 0])
                qb = proj[:, AB_QB:AB_QB + H_B * D_B].reshape(batch, 1, H_B * D_B)
                hb = _sb_decode(qb, past['sb_k'], past['sb_v'], past['page_table'], e)
                hb = hb.reshape(batch, H_B * D_B)
            h = _resid_matmul(h, [ha, hb], [w_out_a, w_out_b])
            new['sb_k'].append(kb.reshape(batch, seq, H_B, D_B))
            new['sb_v'].append(vb.reshape(batch, seq, H_B, D_B))
            new['c'].append(c_new)
            new['n'].append(n_new)
            new['m'].append(m_new)
        else:
            o = i // 2
            w_in_ext, w_q_ext, w_k, w_vt, w_uk_t, w_uv_h = wts['mla'][o]
            proj = _norm_matmul(h, wts['norm_mix'][i], w_in_ext, F32)
            g_q, g_kv = wts['g_q_lora'][o], wts['g_kv_lora'][o]
            if past is None:
                q, ckv, kpe2, kn, vt = _mla_up(proj, g_q, g_kv, w_q_ext, w_k, w_vt, cos, sin, True)
                att = _mla_prompt(q, kn, vt, kpe2, batch, seq)
            else:
                q, ckv, kpe2 = _mla_up(proj, g_q, g_kv, w_q_ext, w_k, w_vt, cos, sin, False)
                q_lat = _head_matmul(q, w_uk_t, NOPE_DIM).reshape(batch, H_C, KV_LORA)
                q_pe = q[:, QU_PE:].reshape(batch, H_C, ROPE_DIM)
                o_lat = _mla_decode(q_lat, q_pe, ckv.reshape(batch, 1, KV_LORA),
                                    kpe2[:, :ROPE_DIM].reshape(batch, 1, ROPE_DIM),
                                    past['ckv'], past['kpe'], past['page_table'], o)
                att = _head_matmul(o_lat.reshape(batch, H_C * KV_LORA), w_uv_h, KV_LORA)
            h = _resid_matmul(h, [att], [wts['w_out_mla'][o]])
            new['ckv'].append(ckv.reshape(batch, seq, KV_LORA))
            new['kpe'].append(kpe2[:, :ROPE_DIM].reshape(batch, seq, ROPE_DIM))
        h = _ffn(h, wts['norm_ffn'][i], wts['w_ffn_gate'][i], wts['w_ffn_up'][i], wts['w_ffn_down'][i])
        h = _ple(h, ple[i].reshape(m, -1), wts['norm_ple'][i], wts['w_ple_gate'][i], wts['w_ple_proj'][i],
                 wts['norm_final'], final=(i == depth - 1))
    return h.reshape(batch, seq, d), {name: jnp.stack(rows) for name, rows in new.items()}


def kernel(x_prompt, x_sample, p_prompt, p_sample, cache_sb_k, cache_sb_v, cache_mla_ckv, cache_mla_kpe,
           state_mlstm_c, state_mlstm_n, state_mlstm_m, page_table, norm_mix, norm_ffn, norm_ple, norm_final,
           w_in_ab, b_gates_ab, g_mlstm_head, w_out_ab, w_in_mla, g_q_lora, g_kv_lora, w_q_up, w_kv_up,
           w_out_mla, w_ffn_gate, w_ffn_up, w_ffn_down, w_ple_gate, w_ple_proj):
    wts = {
        'norm_mix': norm_mix, 'norm_ffn': norm_ffn, 'norm_ple': norm_ple, 'norm_final': norm_final,
        'b_gates_ab': b_gates_ab, 'g_mlstm_head': g_mlstm_head, 'g_q_lora': g_q_lora, 'g_kv_lora': g_kv_lora,
        'ab': [_prep_ab(w_in_ab[e], w_out_ab[e]) for e in range(w_in_ab.shape[0])],
        'mla': [_prep_mla(w_in_mla[o], w_q_up[o], w_kv_up[o]) for o in range(w_in_mla.shape[0])],
        'w_out_mla': [w.astype(BF16) for w in w_out_mla],
        'w_ffn_gate': [w.astype(BF16) for w in w_ffn_gate], 'w_ffn_up': [w.astype(BF16) for w in w_ffn_up],
        'w_ffn_down': [w.astype(BF16) for w in w_ffn_down],
        'w_ple_gate': [w.astype(BF16) for w in w_ple_gate], 'w_ple_proj': [w.astype(BF16) for w in w_ple_proj],
    }
    pos_prompt = jnp.arange(x_prompt.shape[1], dtype=jnp.int32)
    y_prompt, sp = _trunk(x_prompt, p_prompt, pos_prompt, wts, None)
    past_len = page_table.shape[1] * cache_sb_k.shape[2]
    past = {'sb_k': cache_sb_k, 'sb_v': cache_sb_v, 'ckv': cache_mla_ckv, 'kpe': jnp.swapaxes(cache_mla_kpe, 2, 3),
            'c': state_mlstm_c, 'n': state_mlstm_n, 'm': state_mlstm_m, 'page_table': page_table}
    pos_sample = past_len + jnp.arange(x_sample.shape[1], dtype=jnp.int32)
    y_sample, ss = _trunk(x_sample, p_sample, pos_sample, wts, past)
    return (y_prompt, y_sample,
            sp['sb_k'], sp['sb_v'], sp['c'], sp['n'], sp['m'], sp['ckv'], sp['kpe'],
            ss['sb_k'], ss['sb_v'], ss['c'], ss['n'], ss['m'], ss['ckv'], ss['kpe'])
```

```python
import functools

import jax
import jax.numpy as jnp
from jax import lax
from jax.experimental import pallas as pl
from jax.experimental.pallas import tpu as pltpu

F32 = jnp.float32
BF16 = jnp.bfloat16

EPS = 1e-6
H_A, DQK_A, DV_A, MLSTM_CHUNK = 4, 128, 256, 64
H_B, D_B = 8, 128
H_C, Q_LORA, KV_LORA, NOPE_DIM, ROPE_DIM, V_C = 16, 512, 512, 128, 64, 128
ROPE_THETA = 10000.0
SB_SCALE = D_B ** -0.5
MLA_SCALE = (NOPE_DIM + ROPE_DIM) ** -0.5
LANES = 128
SB_DONE = -104.0

AB_QA, AB_KA, AB_VA, AB_OA = 0, 512, 1024, 2048
AB_QB, AB_KB, AB_VB, AB_GATES = 3072, 4096, 5120, 6144
AB_WIDTH = AB_GATES + H_A * LANES
C_CQ, C_CKV, C_KPE, C_KROT = 0, 512, 1024, 1152
C_WIDTH = 1280
QU_NOPE, QU_PE, QU_ROT = 0, H_C * NOPE_DIM, H_C * NOPE_DIM + H_C * ROPE_DIM
QU_WIDTH = QU_ROT + H_C * ROPE_DIM
Q_WIDTH = QU_ROT

VMEM_LIMIT = 56 * 1024 * 1024


def _cparams(*sem):
    return pltpu.CompilerParams(dimension_semantics=sem, vmem_limit_bytes=VMEM_LIMIT)


def _dot(a, b):
    return jnp.dot(a, b, preferred_element_type=F32)


def _dot_nt(a, b):
    return lax.dot_general(a, b, (((1,), (1,)), ((), ())), preferred_element_type=F32)


def _dot_tn(a, b):
    return lax.dot_general(a, b, (((0,), (0,)), ((), ())), preferred_element_type=F32)


def _rms(x, g):
    return x * lax.rsqrt(jnp.mean(x * x, axis=-1, keepdims=True) + EPS) * g


def _softplus_neg_abs(z):
    return jnp.log(1.0 + jnp.exp(-jnp.abs(z)))


def _log_sigmoid(z):
    return jnp.minimum(z, 0.0) - _softplus_neg_abs(z)


def _sigmoid(z):
    return 1.0 / (1.0 + jnp.exp(-z))


def _tile(n, pref, align=8):
    if n <= pref:
        return n
    for t in range(pref - pref % align, 0, -align):
        if n % t == 0:
            return t
    raise ValueError((n, pref, align))


def _norm_matmul_body(x_ref, g_ref, w_ref, o_ref, xn_ref):
    @pl.when(pl.program_id(1) == 0)
    def _():
        xn_ref[...] = _rms(x_ref[...], g_ref[...]).astype(BF16)

    o_ref[...] = _dot(xn_ref[...], w_ref[...]).astype(o_ref.dtype)


def _norm_matmul(x, g, w, out_dtype, tm=512, tn=640):
    m, k = x.shape
    n = w.shape[1]
    tm, tn = _tile(m, tm), _tile(n, tn, LANES)
    return pl.pallas_call(
        _norm_matmul_body,
        grid=(m // tm, n // tn),
        in_specs=[pl.BlockSpec((tm, k), lambda i, j: (i, 0)),
                  pl.BlockSpec((1, k), lambda i, j: (0, 0)),
                  pl.BlockSpec((k, tn), lambda i, j: (0, j))],
        out_specs=pl.BlockSpec((tm, tn), lambda i, j: (i, j)),
        out_shape=jax.ShapeDtypeStruct((m, n), out_dtype),
        scratch_shapes=[pltpu.VMEM((tm, k), BF16)],
        compiler_params=_cparams("parallel", "arbitrary"),
        name="norm_matmul",
    )(x, g.reshape(1, k), w)


def _resid_matmul_body(*refs, n_in):
    h_ref, o_ref = refs[0], refs[-1]
    acc = h_ref[...]
    for i in range(n_in):
        acc = acc + _dot(refs[1 + 2 * i][...].astype(BF16), refs[2 + 2 * i][...])
    o_ref[...] = acc


def _resid_matmul(h, xs, ws, tm=512, tn=1024):
    m, n = h.shape
    tm, tn = _tile(m, tm), _tile(n, tn, LANES)
    in_specs = [pl.BlockSpec((tm, tn), lambda i, j: (i, j))]
    args = [h]
    for x, w in zip(xs, ws):
        k = x.shape[1]
        in_specs += [pl.BlockSpec((tm, k), lambda i, j: (i, 0)),
                     pl.BlockSpec((k, tn), lambda i, j: (0, j))]
        args += [x, w]
    return pl.pallas_call(
        functools.partial(_resid_matmul_body, n_in=len(xs)),
        grid=(m // tm, n // tn),
        in_specs=in_specs,
        out_specs=pl.BlockSpec((tm, tn), lambda i, j: (i, j)),
        out_shape=jax.ShapeDtypeStruct((m, n), F32),
        compiler_params=_cparams("parallel", "arbitrary"),
        name="resid_matmul",
    )(*args)


def _ffn_body(h_ref, g_ref, wg_ref, wu_ref, wd_ref, o_ref, xn_ref):
    @pl.when(pl.program_id(1) == 0)
    def _():
        h = h_ref[...]
        xn_ref[...] = _rms(h, g_ref[...]).astype(BF16)
        o_ref[...] = h

    xn = xn_ref[...]
    a = _dot(xn, wg_ref[...])
    u = _dot(xn, wu_ref[...])
    act = (a * _sigmoid(a) * u).astype(BF16)
    o_ref[...] += _dot(act, wd_ref[...])


def _ffn(h, g, wg, wu, wd, tm=512, tf=512):
    m, d = h.shape
    f = wg.shape[1]
    tm, tf = _tile(m, tm), _tile(f, tf, LANES)
    return pl.pallas_call(
        _ffn_body,
        grid=(m // tm, f // tf),
        in_specs=[pl.BlockSpec((tm, d), lambda i, j: (i, 0)),
                  pl.BlockSpec((1, d), lambda i, j: (0, 0)),
                  pl.BlockSpec((d, tf), lambda i, j: (0, j)),
                  pl.BlockSpec((d, tf), lambda i, j: (0, j)),
                  pl.BlockSpec((tf, d), lambda i, j: (j, 0))],
        out_specs=pl.BlockSpec((tm, d), lambda i, j: (i, 0)),
        out_shape=jax.ShapeDtypeStruct((m, d), F32),
        scratch_shapes=[pltpu.VMEM((tm, d), BF16)],
        compiler_params=_cparams("parallel", "arbitrary"),
        name="ffn",
    )(h, g.reshape(1, d), wg, wu, wd)


def _ple_body(h_ref, p_ref, g_ref, wg_ref, wp_ref, gf_ref, o_ref, *, final):
    h = h_ref[...]
    xn = _rms(h, g_ref[...]).astype(BF16)
    gate = _sigmoid(_dot(xn, wg_ref[...]))
    out = h + gate * _dot(p_ref[...].astype(BF16), wp_ref[...])
    if final:
        out = _rms(out, gf_ref[...])
    o_ref[...] = out


def _ple(h, p, g, wg, wp, g_final, final, tm=512):
    m, d = h.shape
    pd = p.shape[1]
    tm = _tile(m, tm)
    return pl.pallas_call(
        functools.partial(_ple_body, final=final),
        grid=(m // tm,),
        in_specs=[pl.BlockSpec((tm, d), lambda i: (i, 0)),
                  pl.BlockSpec((tm, pd), lambda i: (i, 0)),
                  pl.BlockSpec((1, d), lambda i: (0, 0)),
                  pl.BlockSpec((d, d), lambda i: (0, 0)),
                  pl.BlockSpec((pd, d), lambda i: (0, 0)),
                  pl.BlockSpec((1, d), lambda i: (0, 0))],
        out_specs=pl.BlockSpec((tm, d), lambda i: (i, 0)),
        out_shape=jax.ShapeDtypeStruct((m, d), F32),
        compiler_params=_cparams("parallel"),
        name="ple",
    )(h, p, g.reshape(1, d), wg, wp, g_final.reshape(1, d))


def _mlstm_prompt_body(bias_ref, q_ref, k_ref, v_ref, oa_ref, gc_ref, gr_ref, gh_ref,
                       ha_ref, c_ref, n_ref, m_ref, cs, ns, ms, *, tb, chunk):
    h = pl.program_id(1)
    t = pl.program_id(2)

    @pl.when(t == 0)
    def _():
        cs[...] = jnp.zeros_like(cs)
        ns[...] = jnp.zeros_like(ns)
        ms[...] = jnp.zeros_like(ms)

    b_ig = bias_ref[h]
    b_fg = bias_ref[H_A + h]
    lane = lax.broadcasted_iota(jnp.int32, (tb, LANES), 1)
    gcb = gc_ref[...] + jnp.where(lane == 0, b_ig, b_fg)
    lsg_c = _log_sigmoid(gcb)
    row = lax.broadcasted_iota(jnp.int32, (8, tb), 0)
    grb = gr_ref[...] + jnp.where(row == 0, b_ig, b_fg)
    lsg_r = _log_sigmoid(grb)
    r = lax.broadcasted_iota(jnp.int32, (tb, tb), 0)
    c = lax.broadcasted_iota(jnp.int32, (tb, tb), 1)
    shift = chunk.bit_length() - 1
    same = jnp.right_shift(r, shift) == jnp.right_shift(c, shift)
    tril = jnp.where(same, jnp.where(r >= c, 1.0, 0.0), 0.0).astype(F32)
    bc_all = jnp.dot(tril, lsg_c, precision=lax.Precision.HIGHEST, preferred_element_type=F32)
    br_all = lax.dot_general(lsg_r, tril, (((1,), (1,)), ((), ())),
                             precision=lax.Precision.HIGHEST, preferred_element_type=F32)
    tt = lax.broadcasted_iota(jnp.int32, (chunk, chunk), 0)
    ss = lax.broadcasted_iota(jnp.int32, (chunk, chunk), 1)
    causal = ss <= tt
    gh = gh_ref[...]

    for ci in range(tb // chunk):
        lo, hi = ci * chunk, (ci + 1) * chunk
        bcol = bc_all[lo:hi, 1:2]
        igcol = gcb[lo:hi, 0:1]
        brow = br_all[1:2, lo:hi]
        igrow = grb[0:1, lo:hi]
        m_prev = ms[...]
        d = jnp.where(causal, bcol - brow + igrow, -jnp.inf)
        g = bcol + m_prev
        m_t = jnp.maximum(g, jnp.max(d, axis=-1, keepdims=True))
        w_in = jnp.exp(g - m_t)
        qf = q_ref[lo:hi, :]
        qb = qf.astype(BF16)
        kf = k_ref[lo:hi, :] * (DQK_A ** -0.5)
        kb = kf.astype(BF16)
        vb = v_ref[lo:hi, :].astype(BF16)
        s = _dot_nt(qb, kb) * jnp.exp(d - m_t)
        cmat = cs[...]
        nvec = ns[...]
        num = w_in * _dot(qb, cmat.astype(BF16)) + _dot(s.astype(BF16), vb)
        den = w_in * jnp.sum(qf * nvec, axis=-1, keepdims=True) + jnp.sum(s, axis=-1, keepdims=True)
        hh = num / jnp.maximum(jnp.abs(den), jnp.exp(-m_t))
        m_new = m_t[chunk - 1:chunk, :]
        decay = jnp.exp(g[chunk - 1:chunk, :] - m_new)
        wk = jnp.exp(bcol[chunk - 1:chunk, :] - bcol + igcol - m_new)
        kw = kf * wk
        cs[...] = decay * cmat + _dot_tn(kw.astype(BF16), vb)
        ns[...] = decay * nvec + jnp.sum(kw, axis=0, keepdims=True)
        ms[...] = m_new
        hn = _rms(hh, gh)
        ha_ref[lo:hi, :] = (hn * _sigmoid(oa_ref[lo:hi, :])).astype(ha_ref.dtype)

    @pl.when(t == pl.num_programs(2) - 1)
    def _():
        c_ref[...] = cs[...]
        n_ref[...] = ns[...]
        m_ref[...] = ms[...]


def _mlstm_prompt(proj, gates_t, b_gates, g_head, batch, seq):
    chunk = MLSTM_CHUNK if seq % MLSTM_CHUNK == 0 else seq
    tb = _tile(seq, 512) if seq % MLSTM_CHUNK == 0 else seq
    nt = seq // tb
    m = batch * seq
    qo, ko, vo, oo, go = (AB_QA // DQK_A, AB_KA // DQK_A, AB_VA // DV_A, AB_OA // DV_A, AB_GATES // LANES)
    return pl.pallas_call(
        functools.partial(_mlstm_prompt_body, tb=tb, chunk=chunk),
        grid=(batch, H_A, nt),
        in_specs=[pl.BlockSpec(memory_space=pltpu.SMEM),
                  pl.BlockSpec((tb, DQK_A), lambda b, h, t: (b * nt + t, qo + h)),
                  pl.BlockSpec((tb, DQK_A), lambda b, h, t: (b * nt + t, ko + h)),
                  pl.BlockSpec((tb, DV_A), lambda b, h, t: (b * nt + t, vo + h)),
                  pl.BlockSpec((tb, DV_A), lambda b, h, t: (b * nt + t, oo + h)),
                  pl.BlockSpec((tb, LANES), lambda b, h, t: (b * nt + t, go + h)),
                  pl.BlockSpec((None, 8, tb), lambda b, h, t: (h, 0, b * nt + t)),
                  pl.BlockSpec((1, DV_A), lambda b, h, t: (0, h))],
        out_specs=[pl.BlockSpec((tb, DV_A), lambda b, h, t: (b * nt + t, h)),
                   pl.BlockSpec((None, None, DQK_A, DV_A), lambda b, h, t: (b, h, 0, 0)),
                   pl.BlockSpec((None, None, 1, DQK_A), lambda b, h, t: (b, h, 0, 0)),
                   pl.BlockSpec((None, None, 1, 1), lambda b, h, t: (b, h, 0, 0))],
        out_shape=[jax.ShapeDtypeStruct((m, H_A * DV_A), BF16),
                   jax.ShapeDtypeStruct((batch, H_A, DQK_A, DV_A), F32),
                   jax.ShapeDtypeStruct((batch, H_A, 1, DQK_A), F32),
                   jax.ShapeDtypeStruct((batch, H_A, 1, 1), F32)],
        scratch_shapes=[pltpu.VMEM((DQK_A, DV_A), F32), pltpu.VMEM((1, DQK_A), F32), pltpu.VMEM((1, 1), F32)],
        compiler_params=_cparams("parallel", "parallel", "arbitrary"),
        name="mlstm_prompt",
    )(b_gates, proj, proj, proj, proj, proj, gates_t, g_head.reshape(1, H_A * DV_A))


def _mlstm_decode_body(bias_ref, q_ref, k_ref, v_ref, oa_ref, gc_ref, gh_ref, c0_ref, n0_ref, m0_ref,
                       ha_ref, c_ref, n_ref, m_ref, *, tb):
    h = pl.program_id(1)
    b_ig = bias_ref[h]
    b_fg = bias_ref[H_A + h]
    gc = gc_ref[...]
    ig = gc[:, 0:1] + b_ig
    fg = _log_sigmoid(gc[:, 1:2] + b_fg)
    m0 = m0_ref[...]
    g = fg + m0
    m_t = jnp.maximum(g, ig)
    w_in = jnp.exp(g - m_t)
    e_ig = jnp.exp(ig - m_t)
    qf = q_ref[...]
    kf = k_ref[...] * (DQK_A ** -0.5)
    vf = v_ref[...]
    vb = vf.astype(BF16)
    qk = jnp.sum(qf.astype(BF16).astype(F32) * kf.astype(BF16).astype(F32), axis=-1, keepdims=True)
    s = qk * e_ig
    kw = kf * e_ig
    rows = lax.broadcasted_iota(jnp.int32, (tb, 1), 0)

    def step(b, hq):
        sel = rows == b
        cb = c0_ref[b]
        hq = hq + _dot(jnp.where(sel, qf, 0.0).astype(BF16), cb.astype(BF16))
        upd = _dot_tn(jnp.where(sel, kw, 0.0).astype(BF16), vb)
        decay = jnp.sum(jnp.where(sel, w_in, 0.0), axis=0, keepdims=True)
        c_ref[b] = decay * cb + upd
        return hq

    hq = lax.fori_loop(0, tb, step, jnp.zeros((tb, DV_A), F32))
    n0 = n0_ref[...]
    num = w_in * hq + s * vb.astype(F32)
    den = w_in * jnp.sum(qf * n0, axis=-1, keepdims=True) + s
    hh = num / jnp.maximum(jnp.abs(den), jnp.exp(-m_t))
    n_ref[...] = w_in * n0 + kw
    m_ref[...] = m_t
    hn = _rms(hh, gh_ref[...])
    ha_ref[...] = (hn * _sigmoid(oa_ref[...])).astype(ha_ref.dtype)


def _mlstm_decode(proj, b_gates, g_head, c0, n0, m0, layer):
    batch = proj.shape[0]
    tb = _tile(batch, 32)
    qo, ko, vo, oo, go = (AB_QA // DQK_A, AB_KA // DQK_A, AB_VA // DV_A, AB_OA // DV_A, AB_GATES // LANES)
    return pl.pallas_call(
        functools.partial(_mlstm_decode_body, tb=tb),
        grid=(batch // tb, H_A),
        in_specs=[pl.BlockSpec(memory_space=pltpu.SMEM),
                  pl.BlockSpec((tb, DQK_A), lambda i, h: (i, qo + h)),
                  pl.BlockSpec((tb, DQK_A), lambda i, h: (i, ko + h)),
                  pl.BlockSpec((tb, DV_A), lambda i, h: (i, vo + h)),
                  pl.BlockSpec((tb, DV_A), lambda i, h: (i, oo + h)),
                  pl.BlockSpec((tb, LANES), lambda i, h: (i, go + h)),
                  pl.BlockSpec((1, DV_A), lambda i, h: (0, h)),
                  pl.BlockSpec((None, tb, None, DQK_A, DV_A), lambda i, h: (layer, i, h, 0, 0)),
                  pl.BlockSpec((tb, DQK_A), lambda i, h: (i, h)),
                  pl.BlockSpec((None, tb, 1), lambda i, h: (h, i, 0))],
        out_specs=[pl.BlockSpec((tb, DV_A), lambda i, h: (i, h)),
                   pl.BlockSpec((tb, None, DQK_A, DV_A), lambda i, h: (i, h, 0, 0)),
                   pl.BlockSpec((tb, DQK_A), lambda i, h: (i, h)),
                   pl.BlockSpec((None, tb, 1), lambda i, h: (h, i, 0))],
        out_shape=[jax.ShapeDtypeStruct((batch, H_A * DV_A), BF16),
                   jax.ShapeDtypeStruct((batch, H_A, DQK_A, DV_A), F32),
                   jax.ShapeDtypeStruct((batch, H_A * DQK_A), F32),
                   jax.ShapeDtypeStruct((H_A, batch, 1), F32)],
        compiler_params=_cparams("parallel", "parallel"),
        name="mlstm_decode",
    )(b_gates, proj, proj, proj, proj, proj, g_head.reshape(1, H_A * DV_A), c0, n0, m0)


def _sb_weights(z, mask, tri, carry):
    sp = _softplus_neg_abs(z)
    ls = jnp.minimum(z, 0.0) - sp
    lk = -(jnp.maximum(z, 0.0) + sp)
    if mask is not None:
        lk = jnp.where(mask, lk, 0.0)
    hi = lk.astype(BF16)
    lo = (lk - hi.astype(F32)).astype(BF16)
    la = _dot(hi, tri) + _dot(lo, tri) + carry
    a = jnp.exp(ls + la)
    if mask is not None:
        a = jnp.where(mask, a, 0.0)
    return a, carry + jnp.sum(lk, axis=-1, keepdims=True)


def _later_key_matrix(n):
    r = lax.broadcasted_iota(jnp.int32, (n, n), 0)
    c = lax.broadcasted_iota(jnp.int32, (n, n), 1)
    return jnp.where(r > c, 1.0, 0.0).astype(BF16)


def _sb_prompt_body(q_ref, k_ref, v_ref, o_ref, acc_ref, carry_ref, done_ref, *, tq):
    i = pl.program_id(1)
    jj = pl.program_id(2)
    j = i - jj

    @pl.when(jj == 0)
    def _():
        acc_ref[...] = jnp.zeros_like(acc_ref)
        carry_ref[...] = jnp.zeros_like(carry_ref)
        done_ref[0] = 0

    @pl.when(jnp.logical_and(j >= 0, done_ref[0] == 0))
    def _():
        qpos = i * tq + lax.broadcasted_iota(jnp.int32, (tq, tq), 0)
        kpos = j * tq + lax.broadcasted_iota(jnp.int32, (tq, tq), 1)
        mask = kpos < qpos
        tri = _later_key_matrix(tq)
        for h in range(H_B):
            sl = slice(h * D_B, (h + 1) * D_B)
            z = _dot_nt(q_ref[:, sl].astype(BF16), k_ref[:, sl].astype(BF16)) * SB_SCALE
            a, carry = _sb_weights(z, mask, tri, carry_ref[h])
            acc_ref[h] += _dot(a.astype(BF16), v_ref[:, sl].astype(BF16))
            carry_ref[h] = carry
        done_ref[0] = (jnp.max(carry_ref[...]) < SB_DONE).astype(jnp.int32)

    @pl.when(jj == pl.num_programs(2) - 1)
    def _():
        for h in range(H_B):
            o_ref[:, h * D_B:(h + 1) * D_B] = acc_ref[h].astype(o_ref.dtype)


def _sb_prompt(proj, batch, seq):
    tq = _tile(seq, 256)
    nq = seq // tq
    width = H_B * D_B
    qo, ko, vo = AB_QB // width, AB_KB // width, AB_VB // width
    return pl.pallas_call(
        functools.partial(_sb_prompt_body, tq=tq),
        grid=(batch, nq, nq),
        in_specs=[pl.BlockSpec((tq, width), lambda b, i, jj: (b * nq + i, qo)),
                  pl.BlockSpec((tq, width), lambda b, i, jj: (b * nq + jnp.maximum(i - jj, 0), ko)),
                  pl.BlockSpec((tq, width), lambda b, i, jj: (b * nq + jnp.maximum(i - jj, 0), vo))],
        out_specs=pl.BlockSpec((tq, width), lambda b, i, jj: (b * nq + i, 0)),
        out_shape=jax.ShapeDtypeStruct((batch * seq, width), BF16),
        scratch_shapes=[pltpu.VMEM((H_B, tq, D_B), F32), pltpu.VMEM((H_B, tq, 1), F32),
                        pltpu.SMEM((1,), jnp.int32)],
        compiler_params=_cparams("parallel", "parallel", "arbitrary"),
        name="sb_prompt",
    )(proj, proj, proj)


def _sb_decode_body(pt_ref, q_ref, k_hbm, v_hbm, o_ref, fk, fv, kb, vb, fsem, bsem, acc_ref, carry_ref,
                    *, n_pages, page, layer):
    b = pl.program_id(0)
    nb = pl.num_programs(0)

    def head_copies(src, dst, sem):
        return [pltpu.make_async_copy(src.at[:, h, :], dst.at[h], sem) for h in range(H_B)]

    def first_copies(seq, slot):
        pg = pt_ref[seq, n_pages - 1]
        return (head_copies(k_hbm.at[layer, pg], fk.at[slot], fsem.at[0, slot])
                + head_copies(v_hbm.at[layer, pg], fv.at[slot], fsem.at[1, slot]))

    def page_copies(p, slot):
        pg = pt_ref[b, p]
        return (head_copies(k_hbm.at[layer, pg], kb.at[slot], bsem.at[0, slot])
                + head_copies(v_hbm.at[layer, pg], vb.at[slot], bsem.at[1, slot]))

    @pl.when(b == 0)
    def _():
        for c in first_copies(0, 0):
            c.start()

    @pl.when(b + 1 < nb)
    def _():
        for c in first_copies(b + 1, (b + 1) % 2):
            c.start()

    if n_pages > 1:
        for c in page_copies(n_pages - 2, 0):
            c.start()

    acc_ref[...] = jnp.zeros_like(acc_ref)
    carry_ref[...] = jnp.zeros_like(carry_ref)
    q = q_ref[...]
    qall = jnp.concatenate([q[:, h * D_B:(h + 1) * D_B] for h in range(H_B)], axis=0).astype(BF16)
    head = lax.broadcasted_iota(jnp.int32, (H_B, page), 0)
    head_d = lax.broadcasted_iota(jnp.int32, (H_B, D_B), 0)
    tri = _later_key_matrix(page)

    def sweep_page(k_ref, v_ref):
        z = jnp.zeros((H_B, page), F32)
        for h in range(H_B):
            z = jnp.where(head == h, _dot_nt(qall, k_ref[h].astype(BF16)), z)
        a, carry = _sb_weights(z * SB_SCALE, None, tri, carry_ref[...])
        ab = a.astype(BF16)
        upd = jnp.zeros((H_B, D_B), F32)
        for h in range(H_B):
            upd = jnp.where(head_d == h, _dot(ab, v_ref[h].astype(BF16)), upd)
        acc_ref[...] += upd
        carry_ref[...] = carry
        return jnp.max(carry) < SB_DONE

    fslot = b % 2
    for c in first_copies(b, fslot):
        c.wait()
    done0 = sweep_page(fk.at[fslot], fv.at[fslot])

    def more(st):
        p, _, done = st
        return jnp.logical_and(p >= 0, jnp.logical_not(done))

    def step(st):
        p, slot, _ = st
        for c in page_copies(p, slot):
            c.wait()

        @pl.when(p >= 1)
        def _():
            for c in page_copies(p - 1, 1 - slot):
                c.start()

        return p - 1, 1 - slot, sweep_page(kb.at[slot], vb.at[slot])

    p, slot, _ = lax.while_loop(more, step, (jnp.int32(n_pages - 2), jnp.int32(0), done0))

    @pl.when(p >= 0)
    def _():
        for c in page_copies(p, slot):
            c.wait()

    for h in range(H_B):
        o_ref[:, h * D_B:(h + 1) * D_B] = acc_ref[h:h + 1, :].astype(o_ref.dtype)


def _sb_decode(q, cache_k, cache_v, page_table, layer):
    batch = q.shape[0]
    n_pages = page_table.shape[1]
    page = cache_k.shape[2]
    page_buf = pltpu.VMEM((2, H_B, page, D_B), cache_k.dtype)
    grid_spec = pltpu.PrefetchScalarGridSpec(
        num_scalar_prefetch=1,
        grid=(batch,),
        in_specs=[pl.BlockSpec((None, 1, H_B * D_B), lambda b, pt: (b, 0, 0)),
                  pl.BlockSpec(memory_space=pl.ANY),
                  pl.BlockSpec(memory_space=pl.ANY)],
        out_specs=pl.BlockSpec((None, 1, H_B * D_B), lambda b, pt: (b, 0, 0)),
        scratch_shapes=[page_buf, page_buf, page_buf, page_buf,
                        pltpu.SemaphoreType.DMA((2, 2)), pltpu.SemaphoreType.DMA((2, 2)),
                        pltpu.VMEM((H_B, D_B), F32), pltpu.VMEM((H_B, 1), F32)],
    )
    return pl.pallas_call(
        functools.partial(_sb_decode_body, n_pages=n_pages, page=page, layer=layer),
        grid_spec=grid_spec,
        out_shape=jax.ShapeDtypeStruct((batch, 1, H_B * D_B), BF16),
        compiler_params=_cparams("arbitrary"),
        name="sb_decode",
    )(page_table, q, cache_k, cache_v)


def _mla_up_body(pr_ref, gq_ref, gkv_ref, wq_ref, wk_ref, wvt_ref, cos_ref, sin_ref, *out_refs, with_kv):
    q_ref, ckv_ref, kpe_ref = out_refs[:3]
    cos = cos_ref[...]
    sin = sin_ref[...]
    cq = _rms(pr_ref[:, C_CQ:C_CQ + Q_LORA], gq_ref[...]).astype(BF16)
    ckv = _rms(pr_ref[:, C_CKV:C_CKV + KV_LORA], gkv_ref[...])
    ckv_ref[...] = ckv
    kpe_ref[...] = pr_ref[:, C_KPE:C_KPE + LANES] * cos + pr_ref[:, C_KROT:C_KROT + LANES] * sin
    qa = _dot(cq, wq_ref[...])
    q_ref[:, QU_NOPE:QU_PE] = qa[:, QU_NOPE:QU_PE].astype(BF16)
    for c in range((QU_ROT - QU_PE) // LANES):
        pe = qa[:, QU_PE + c * LANES:QU_PE + (c + 1) * LANES]
        rot = qa[:, QU_ROT + c * LANES:QU_ROT + (c + 1) * LANES]
        q_ref[:, QU_PE + c * LANES:QU_PE + (c + 1) * LANES] = (pe * cos + rot * sin).astype(BF16)
    if with_kv:
        ckv_b = ckv.astype(BF16)
        out_refs[3][...] = _dot(ckv_b, wk_ref[...]).astype(BF16)
        out_refs[4][...] = _dot_nt(wvt_ref[...], ckv_b).astype(BF16)


def _mla_up(proj, g_q, g_kv, w_q, w_k, w_vt, cos, sin, with_kv, tm=256):
    m = proj.shape[0]
    tm = _tile(m, tm)
    k_w, v_w = w_k.shape[1], w_vt.shape[0]
    out_specs = [pl.BlockSpec((tm, Q_WIDTH), lambda i: (i, 0)),
                 pl.BlockSpec((tm, KV_LORA), lambda i: (i, 0)),
                 pl.BlockSpec((tm, LANES), lambda i: (i, 0))]
    out_shape = [jax.ShapeDtypeStruct((m, Q_WIDTH), BF16),
                 jax.ShapeDtypeStruct((m, KV_LORA), F32),
                 jax.ShapeDtypeStruct((m, LANES), F32)]
    if with_kv:
        out_specs += [pl.BlockSpec((tm, k_w), lambda i: (i, 0)), pl.BlockSpec((v_w, tm), lambda i: (0, i))]
        out_shape += [jax.ShapeDtypeStruct((m, k_w), BF16), jax.ShapeDtypeStruct((v_w, m), BF16)]
    return pl.pallas_call(
        functools.partial(_mla_up_body, with_kv=with_kv),
        grid=(m // tm,),
        in_specs=[pl.BlockSpec((tm, C_WIDTH), lambda i: (i, 0)),
                  pl.BlockSpec((1, Q_LORA), lambda i: (0, 0)),
                  pl.BlockSpec((1, KV_LORA), lambda i: (0, 0)),
                  pl.BlockSpec((Q_LORA, QU_WIDTH), lambda i: (0, 0)),
                  pl.BlockSpec((KV_LORA, k_w), lambda i: (0, 0)),
                  pl.BlockSpec((v_w, KV_LORA), lambda i: (0, 0)),
                  pl.BlockSpec((tm, LANES), lambda i: (i, 0)),
                  pl.BlockSpec((tm, LANES), lambda i: (i, 0))],
        out_specs=out_specs,
        out_shape=out_shape,
        compiler_params=_cparams("parallel"),
        name="mla_up",
    )(proj, g_q.reshape(1, Q_LORA), g_kv.reshape(1, KV_LORA), w_q, w_k, w_vt, cos, sin)


def _mla_prompt_body(q_ref, kn_ref, vt_ref, kpe_ref, o_ref, m_ref, l_ref, acc_ref, *, tq):
    i = pl.program_id(1)
    j = pl.program_id(2)

    @pl.when(j == 0)
    def _():
        m_ref[...] = jnp.full_like(m_ref, -1e30)
        l_ref[...] = jnp.zeros_like(l_ref)
        acc_ref[...] = jnp.zeros_like(acc_ref)

    @pl.when(j <= i)
    def _():
        kpos = j * tq + lax.broadcasted_iota(jnp.int32, (tq, tq), 0)
        qpos = i * tq + lax.broadcasted_iota(jnp.int32, (tq, tq), 1)
        mask = kpos <= qpos
        kpe = kpe_ref[...]
        lane = lax.broadcasted_iota(jnp.int32, kpe.shape, 1)
        kpe_sel = (jnp.where(lane < ROPE_DIM, kpe, 0.0).astype(BF16),
                   jnp.where(lane >= ROPE_DIM, kpe, 0.0).astype(BF16))
        for h in range(H_C):
            sl = slice(h * NOPE_DIM, (h + 1) * NOPE_DIM)
            pe = slice(QU_PE + (h // 2) * LANES, QU_PE + (h // 2 + 1) * LANES)
            qcat = jnp.concatenate([q_ref[:, sl], q_ref[:, pe]], axis=1)
            kcat = jnp.concatenate([kn_ref[:, sl], kpe_sel[h % 2]], axis=1)
            s = jnp.where(mask, _dot_nt(kcat, qcat) * MLA_SCALE, -jnp.inf)
            m_prev = m_ref[h]
            m_new = jnp.maximum(m_prev, jnp.max(s, axis=0, keepdims=True))
            alpha = jnp.exp(m_prev - m_new)
            p = jnp.exp(s - m_new)
            l_ref[h] = alpha * l_ref[h] + jnp.sum(p, axis=0, keepdims=True)
            acc_ref[h] = alpha * acc_ref[h] + _dot(vt_ref[sl, :], p.astype(BF16))
            m_ref[h] = m_new

    @pl.when(j == pl.num_programs(2) - 1)
    def _():
        for h in range(H_C):
            o_ref[:, h * V_C:(h + 1) * V_C] = jnp.transpose(acc_ref[h] / l_ref[h]).astype(o_ref.dtype)


def _mla_prompt(q, kn, vt, kpe2, batch, seq):
    tq = _tile(seq, 512)
    nq = seq // tq
    kw, vw = H_C * NOPE_DIM, H_C * V_C
    return pl.pallas_call(
        functools.partial(_mla_prompt_body, tq=tq),
        grid=(batch, nq, nq),
        in_specs=[pl.BlockSpec((tq, Q_WIDTH), lambda b, i, j: (b * nq + i, 0)),
                  pl.BlockSpec((tq, kw), lambda b, i, j: (b * nq + jnp.minimum(i, j), 0)),
                  pl.BlockSpec((vw, tq), lambda b, i, j: (0, b * nq + jnp.minimum(i, j))),
                  pl.BlockSpec((tq, LANES), lambda b, i, j: (b * nq + jnp.minimum(i, j), 0))],
        out_specs=pl.BlockSpec((tq, vw), lambda b, i, j: (b * nq + i, 0)),
        out_shape=jax.ShapeDtypeStruct((batch * seq, vw), BF16),
        scratch_shapes=[pltpu.VMEM((H_C, 1, tq), F32), pltpu.VMEM((H_C, 1, tq), F32),
                        pltpu.VMEM((H_C, V_C, tq), F32)],
        compiler_params=_cparams("parallel", "parallel", "arbitrary"),
        name="mla_prompt",
    )(q, kn, vt, kpe2)


def _head_matmul_body(x_ref, w_ref, o_ref):
    o_ref[...] = _dot(x_ref[...], w_ref[...]).astype(o_ref.dtype)


def _head_matmul(x, w, d_in, col0=0):
    m = x.shape[0]
    n_h, _, d_out = w.shape
    off = col0 // d_in
    return pl.pallas_call(
        _head_matmul_body,
        grid=(n_h,),
        in_specs=[pl.BlockSpec((m, d_in), lambda h: (0, off + h)),
                  pl.BlockSpec((None, d_in, d_out), lambda h: (h, 0, 0))],
        out_specs=pl.BlockSpec((m, d_out), lambda h: (0, h)),
        out_shape=jax.ShapeDtypeStruct((m, n_h * d_out), BF16),
        compiler_params=_cparams("parallel"),
        name="head_matmul",
    )(x, w)


def _mla_decode_body(pt_ref, ql_ref, qp_ref, cn_ref, kn_ref, *refs, pages_per_step, page):
    del pt_ref
    g_n = pages_per_step
    c_refs, p_refs = refs[:g_n], refs[g_n:2 * g_n]
    o_ref, kbuf, pbuf, m_ref, l_ref, acc_ref = refs[2 * g_n:]
    jj = pl.program_id(1)
    ql = ql_ref[...]
    qp = qp_ref[...]

    @pl.when(jj == 0)
    def _():
        lat = cn_ref[...].astype(BF16).astype(F32)
        kpe = kn_ref[...].astype(BF16).astype(F32)
        s0 = (jnp.sum(ql.astype(F32) * lat, axis=-1, keepdims=True)
              + jnp.sum(qp.astype(F32) * kpe, axis=-1, keepdims=True)) * MLA_SCALE
        m_ref[...] = s0
        l_ref[...] = jnp.ones_like(l_ref)
        acc_ref[...] = jnp.broadcast_to(lat, acc_ref.shape)

    for g in range(g_n):
        kbuf[g * page:(g + 1) * page, :] = c_refs[g][...].astype(BF16)
        pbuf[:, g * page:(g + 1) * page] = p_refs[g][...].astype(BF16)
    s = (_dot_nt(ql, kbuf[...]) + _dot(qp, pbuf[...])) * MLA_SCALE
    m_prev = m_ref[...]
    m_new = jnp.maximum(m_prev, jnp.max(s, axis=-1, keepdims=True))
    alpha = jnp.exp(m_prev - m_new)
    p = jnp.exp(s - m_new)
    l_ref[...] = alpha * l_ref[...] + jnp.sum(p, axis=-1, keepdims=True)
    acc_ref[...] = alpha * acc_ref[...] + _dot(p.astype(BF16), kbuf[...])
    m_ref[...] = m_new

    @pl.when(jj == pl.num_programs(1) - 1)
    def _():
        o_ref[...] = (acc_ref[...] / l_ref[...]).astype(o_ref.dtype)


def _mla_decode(q_lat, q_pe, ckv_new, kpe_new, cache_ckv, cache_kpe_t, page_table, layer):
    batch = q_lat.shape[0]
    n_pages = page_table.shape[1]
    page = cache_ckv.shape[2]
    g_n = _tile(n_pages, 64, 1)
    steps = n_pages // g_n

    def page_spec(rows, cols, g):
        return pl.BlockSpec((None, None, rows, cols), lambda b, jj, pt: (layer, pt[b, jj * g_n + g], 0, 0))

    grid_spec = pltpu.PrefetchScalarGridSpec(
        num_scalar_prefetch=1,
        grid=(batch, steps),
        in_specs=[pl.BlockSpec((None, H_C, KV_LORA), lambda b, jj, pt: (b, 0, 0)),
                  pl.BlockSpec((None, H_C, ROPE_DIM), lambda b, jj, pt: (b, 0, 0)),
                  pl.BlockSpec((None, 1, KV_LORA), lambda b, jj, pt: (b, 0, 0)),
                  pl.BlockSpec((None, 1, ROPE_DIM), lambda b, jj, pt: (b, 0, 0))]
        + [page_spec(page, KV_LORA, g) for g in range(g_n)]
        + [page_spec(ROPE_DIM, page, g) for g in range(g_n)],
        out_specs=pl.BlockSpec((None, H_C, KV_LORA), lambda b, jj, pt: (b, 0, 0)),
        scratch_shapes=[pltpu.VMEM((g_n * page, KV_LORA), BF16),
                        pltpu.VMEM((ROPE_DIM, g_n * page), BF16),
                        pltpu.VMEM((H_C, 1), F32), pltpu.VMEM((H_C, 1), F32),
                        pltpu.VMEM((H_C, KV_LORA), F32)],
    )
    return pl.pallas_call(
        functools.partial(_mla_decode_body, pages_per_step=g_n, page=page),
        grid_spec=grid_spec,
        out_shape=jax.ShapeDtypeStruct((batch, H_C, KV_LORA), BF16),
        compiler_params=_cparams("parallel", "arbitrary"),
        name="mla_decode",
    )(page_table, q_lat, q_pe, ckv_new, kpe_new, *([cache_ckv] * g_n), *([cache_kpe_t] * g_n))


def _rot_cols(w):
    k = w.shape[0]
    w4 = w.reshape(k, -1, 2, ROPE_DIM // 2)
    return jnp.stack([-w4[:, :, 1], w4[:, :, 0]], axis=2).reshape(k, -1)


def _prep_ab(w_in, w_out):
    d = w_in.shape[0]
    sizes = (H_A * DQK_A, H_A * DQK_A, H_A * DV_A, H_A * DV_A, 2 * H_A, H_B * D_B, H_B * D_B, H_B * D_B)
    parts, start = [], 0
    for sz in sizes:
        parts.append(w_in[:, start:start + sz])
        start += sz
    qa, ka, va, oa, gates, qb, kb, vb = parts
    gcols = jnp.zeros((d, H_A, LANES), w_in.dtype)
    gcols = gcols.at[:, :, 0].set(gates[:, :H_A]).at[:, :, 1].set(gates[:, H_A:])
    w_ext = jnp.concatenate([qa, ka, va, oa, qb, kb, vb, gcols.reshape(d, H_A * LANES)], axis=1).astype(BF16)
    w_out = w_out.astype(BF16)
    return w_ext, w_out[:H_A * DV_A], w_out[H_A * DV_A:]


def _prep_mla(w_in, w_q_up, w_kv_up):
    kpe_w = w_in[:, Q_LORA + KV_LORA:]
    rot_w = _rot_cols(kpe_w)
    w_in_ext = jnp.concatenate([w_in[:, :Q_LORA + KV_LORA], kpe_w, kpe_w, rot_w, rot_w], axis=1).astype(BF16)
    wq = w_q_up.reshape(Q_LORA, H_C, NOPE_DIM + ROPE_DIM)
    wq_pe = wq[:, :, NOPE_DIM:].reshape(Q_LORA, H_C * ROPE_DIM)
    w_q_ext = jnp.concatenate([wq[:, :, :NOPE_DIM].reshape(Q_LORA, H_C * NOPE_DIM), wq_pe, _rot_cols(wq_pe)],
                              axis=1).astype(BF16)
    wkv = w_kv_up.reshape(KV_LORA, H_C, NOPE_DIM + V_C)
    w_uk, w_uv = wkv[:, :, :NOPE_DIM], wkv[:, :, NOPE_DIM:]
    w_k = w_uk.reshape(KV_LORA, -1).astype(BF16)
    w_vt = jnp.transpose(w_uv.reshape(KV_LORA, -1)).astype(BF16)
    w_uk_t = jnp.transpose(w_uk, (1, 2, 0)).astype(BF16)
    w_uv_h = jnp.transpose(w_uv, (1, 0, 2)).astype(BF16)
    return w_in_ext, w_q_ext, w_k, w_vt, w_uk_t, w_uv_h


def _rope_tables(pos):
    half = ROPE_DIM // 2
    freqs = ROPE_THETA ** (-jnp.arange(half, dtype=F32) / half)
    ang = pos.astype(F32)[:, None] * freqs[None, :]
    reps = LANES // half
    return jnp.tile(jnp.cos(ang), (1, reps)), jnp.tile(jnp.sin(ang), (1, reps))


def _trunk(x, ple, pos, wts, past):
    batch, seq, d = x.shape
    m = batch * seq
    h = x.reshape(m, d)
    depth = wts['norm_mix'].shape[0]
    cos, sin = _rope_tables(pos)
    cos, sin = jnp.tile(cos, (batch, 1)), jnp.tile(sin, (batch, 1))
    new = {name: [] for name in ('sb_k', 'sb_v', 'c', 'n', 'm', 'ckv', 'kpe')}
    for i in range(depth):
        if i % 2 == 0:
            e = i // 2
            w_ext, w_out_a, w_out_b = wts['ab'][e]
            proj = _norm_matmul(h, wts['norm_mix'][i], w_ext, F32, tn=AB_WIDTH // 4)
            kb = proj[:, AB_KB:AB_KB + H_B * D_B]
            vb = proj[:, AB_VB:AB_VB + H_B * D_B]
            b_gates = wts['b_gates_ab'][e].astype(F32)
            g_head = wts['g_mlstm_head'][e].astype(F32)
            if past is None:
                gates_t = proj[:, AB_GATES:].reshape(m, H_A, LANES)[:, :, :8]
                gates_t = jnp.transpose(gates_t, (1, 2, 0))
                ha, c_new, n_new, m_new = _mlstm_prompt(proj, gates_t, b_gates, g_head, batch, seq)
                n_new = n_new.reshape(batch, H_A, DQK_A)
                m_new = m_new.reshape(batch, H_A)
                hb = _sb_prompt(proj, batch, seq)
            else:
                n0 = past['n'][e].astype(F32).reshape(batch, H_A * DQK_A)
                m0 = jnp.transpose(past['m'][e].astype(F32))[:, :, None]
                ha, c_new, n_new, m_new = _mlstm_decode(proj, b_gates, g_head, past['c'], n0, m0, e)
                n_new = n_new.reshape(batch, H_A, DQK_A)
                m_new = jnp.transpose(m_new[:, :, 0])
                qb = proj[:, AB_QB:AB_QB + H_B * D_B].reshape(batch, 1, H_B * D_B)
                hb = _sb_decode(qb, past['sb_k'], past['sb_v'], past['page_table'], e)
                hb = hb.reshape(batch, H_B * D_B)
            h = _resid_matmul(h, [ha, hb], [w_out_a, w_out_b])
            new['sb_k'].append(kb.reshape(batch, seq, H_B, D_B))
            new['sb_v'].append(vb.reshape(batch, seq, H_B, D_B))
            new['c'].append(c_new)
            new['n'].append(n_new)
            new['m'].append(m_new)
        else:
            o = i // 2
            w_in_ext, w_q_ext, w_k, w_vt, w_uk_t, w_uv_h = wts['mla'][o]
            proj = _norm_matmul(h, wts['norm_mix'][i], w_in_ext, F32)
            g_q, g_kv = wts['g_q_lora'][o], wts['g_kv_lora'][o]
            if past is None:
                q, ckv, kpe2, kn, vt = _mla_up(proj, g_q, g_kv, w_q_ext, w_k, w_vt, cos, sin, True)
                att = _mla_prompt(q, kn, vt, kpe2, batch, seq)
            else:
                q, ckv, kpe2 = _mla_up(proj, g_q, g_kv, w_q_ext, w_k, w_vt, cos, sin, False)
                q_lat = _head_matmul(q, w_uk_t, NOPE_DIM).reshape(batch, H_C, KV_LORA)
                q_pe = q[:, QU_PE:].reshape(batch, H_C, ROPE_DIM)
                o_lat = _mla_decode(q_lat, q_pe, ckv.reshape(batch, 1, KV_LORA),
                                    kpe2[:, :ROPE_DIM].reshape(batch, 1, ROPE_DIM),
                                    past['ckv'], past['kpe'], past['page_table'], o)
                att = _head_matmul(o_lat.reshape(batch, H_C * KV_LORA), w_uv_h, KV_LORA)
            h = _resid_matmul(h, [att], [wts['w_out_mla'][o]])
            new['ckv'].append(ckv.reshape(batch, seq, KV_LORA))
            new['kpe'].append(kpe2[:, :ROPE_DIM].reshape(batch, seq, ROPE_DIM))
        h = _ffn(h, wts['norm_ffn'][i], wts['w_ffn_gate'][i], wts['w_ffn_up'][i], wts['w_ffn_down'][i])
        h = _ple(h, ple[i].reshape(m, -1), wts['norm_ple'][i], wts['w_ple_gate'][i], wts['w_ple_proj'][i],
                 wts['norm_final'], final=(i == depth - 1))
    return h.reshape(batch, seq, d), {name: jnp.stack(rows) for name, rows in new.items()}


def kernel(x_prompt, x_sample, p_prompt, p_sample, cache_sb_k, cache_sb_v, cache_mla_ckv, cache_mla_kpe,
           state_mlstm_c, state_mlstm_n, state_mlstm_m, page_table, norm_mix, norm_ffn, norm_ple, norm_final,
           w_in_ab, b_gates_ab, g_mlstm_head, w_out_ab, w_in_mla, g_q_lora, g_kv_lora, w_q_up, w_kv_up,
           w_out_mla, w_ffn_gate, w_ffn_up, w_ffn_down, w_ple_gate, w_ple_proj):
    wts = {
        'norm_mix': norm_mix, 'norm_ffn': norm_ffn, 'norm_ple': norm_ple, 'norm_final': norm_final,
        'b_gates_ab': b_gates_ab, 'g_mlstm_head': g_mlstm_head, 'g_q_lora': g_q_lora, 'g_kv_lora': g_kv_lora,
        'ab': [_prep_ab(w_in_ab[e], w_out_ab[e]) for e in range(w_in_ab.shape[0])],
        'mla': [_prep_mla(w_in_mla[o], w_q_up[o], w_kv_up[o]) for o in range(w_in_mla.shape[0])],
        'w_out_mla': [w.astype(BF16) for w in w_out_mla],
        'w_ffn_gate': [w.astype(BF16) for w in w_ffn_gate], 'w_ffn_up': [w.astype(BF16) for w in w_ffn_up],
        'w_ffn_down': [w.astype(BF16) for w in w_ffn_down],
        'w_ple_gate': [w.astype(BF16) for w in w_ple_gate], 'w_ple_proj': [w.astype(BF16) for w in w_ple_proj],
    }
    pos_prompt = jnp.arange(x_prompt.shape[1], dtype=jnp.int32)
    y_prompt, sp = _trunk(x_prompt, p_prompt, pos_prompt, wts, None)
    past_len = page_table.shape[1] * cache_sb_k.shape[2]
    past = {'sb_k': cache_sb_k, 'sb_v': cache_sb_v, 'ckv': cache_mla_ckv, 'kpe': jnp.swapaxes(cache_mla_kpe, 2, 3),
            'c': state_mlstm_c, 'n': state_mlstm_n, 'm': state_mlstm_m, 'page_table': page_table}
    pos_sample = past_len + jnp.arange(x_sample.shape[1], dtype=jnp.int32)
    y_sample, ss = _trunk(x_sample, p_sample, pos_sample, wts, past)
    return (y_prompt, y_sample,
            sp['sb_k'], sp['sb_v'], sp['c'], sp['n'], sp['m'], sp['ckv'], sp['kpe'],
            ss['sb_k'], ss['sb_v'], ss['c'], ss['n'], ss['m'], ss['ckv'], ss['kpe'])
```
